```python
import jax, jax.numpy as jnp
from jax import lax
import numpy as np

D_MODEL = 1024
BATCH = 8
SEQ = 2048
DEPTH = 2

A_DILATION_PAIRS = ((128, 1), (512, 4), (2048, 16))
A_GROUPS = len(A_DILATION_PAIRS)
A_HEADS = 4
A_HEAD_DIM = 128
A_QK_WIDTH = A_GROUPS * A_HEADS * A_HEAD_DIM
A_OUT_WIDTH = A_HEADS * A_HEAD_DIM
A_BLOCK = 128
ROPE_THETA = 10000.0
NEG_INF = -1e30
B_WIDTH = 512
B_KERNEL = 31
C_WIDTH = D_MODEL
C_KERNEL = 3
FFN_HIDDEN = -(-8 * D_MODEL // (3 * 256)) * 256
PLE_DIM = 256
N_EVEN = (DEPTH + 1) // 2
N_ODD = DEPTH // 2
DN_ALPHA = float((2 * DEPTH) ** 0.25)
DN_BETA = float((8 * DEPTH) ** -0.25)
LN_EPS = 1e-5
EVEN_IN_WIDTH = 3 * A_QK_WIDTH + 2 * B_WIDTH
EVEN_OUT_WIDTH = A_OUT_WIDTH + B_WIDTH

kernel_name = "hybrid_dilated_attn_conformer_shortconv_deepnorm"


def layer_norm(x, g, b):
    xf = x.astype(jnp.float32)
    mu = jnp.mean(xf, axis=-1, keepdims=True)
    var = jnp.mean(jnp.square(xf - mu), axis=-1, keepdims=True)
    return ((xf - mu) * lax.rsqrt(var + LN_EPS) * g.astype(jnp.float32) + b.astype(jnp.float32)).astype(x.dtype)


def rope(t, pos):
    half = t.shape[-1] // 2
    inv = ROPE_THETA ** (-jnp.arange(half, dtype=jnp.float32) / half)
    ang = pos.astype(jnp.float32)[:, None] * inv[None, :]
    bshape = (1, t.shape[1]) + (1,) * (t.ndim - 3) + (half,)
    cos = jnp.cos(ang).reshape(bshape)
    sin = jnp.sin(ang).reshape(bshape)
    tf = t.astype(jnp.float32)
    t1, t2 = tf[..., :half], tf[..., half:]
    return jnp.concatenate([t1 * cos - t2 * sin, t1 * sin + t2 * cos], axis=-1).astype(t.dtype)


def causal_depthwise_conv(x, w):
    k, c = w.shape
    return lax.conv_general_dilated(
        x, w[:, None, :].astype(x.dtype), window_strides=(1,), padding=[(k - 1, 0)],
        dimension_numbers=("NWC", "WIO", "NWC"), feature_group_count=c)


def to_strided_blocks(t, dil, n_blocks):
    b, s, h, e = t.shape
    length = s // dil
    t = t.reshape(b, length, dil, h, e).transpose(0, 2, 1, 3, 4)
    t = jnp.pad(t, ((0, 0), (0, 0), (0, n_blocks * A_BLOCK - length), (0, 0), (0, 0)))
    return t.reshape(b, dil, n_blocks, A_BLOCK, h, e)


def from_strided_blocks(t, s):
    b, dil, nb, qb = t.shape[:4]
    rest = t.shape[4:]
    t = t.reshape((b, dil, nb * qb) + rest)[:, :, : s // dil]
    t = jnp.moveaxis(t, 1, 2)
    return t.reshape((b, s) + rest)


def dilated_window_branch(q, k, v, window, dil):
    b, s, h, e = q.shape
    sub_w = window // dil
    nb = -(-(s // dil) // A_BLOCK)
    qb = to_strided_blocks(q, dil, nb).astype(jnp.float32)
    kb = to_strided_blocks(k, dil, nb).astype(jnp.float32)
    vb = to_strided_blocks(v, dil, nb).astype(jnp.float32)

    def with_prev(t):
        prev = jnp.pad(t[:, :, :-1], ((0, 0), (0, 0), (1, 0), (0, 0), (0, 0), (0, 0)))
        return jnp.concatenate([prev, t], axis=3)

    kk, vv = with_prev(kb), with_prev(vb)
    sc = jnp.einsum("brnqhe,brnkhe->brnhqk", qb, kk) * (e ** -0.5)
    qi = jnp.arange(A_BLOCK)[:, None]
    kj = jnp.arange(2 * A_BLOCK)[None, :]
    dist = qi + A_BLOCK - kj
    band = (dist >= 0) & (dist <= sub_w)
    valid_prev = (jnp.arange(nb) > 0)[:, None, None] | (kj >= A_BLOCK)[None]
    mask = band[None] & valid_prev
    sc = jnp.where(mask[None, None, :, None], sc, NEG_INF)
    m = jnp.max(sc, axis=-1, keepdims=True)
    ex = jnp.exp(sc - m)
    den = jnp.sum(ex, axis=-1)
    o = jnp.einsum("brnhqk,brnkhe->brnqhe", ex, vv) / jnp.moveaxis(den, -1, -2)[..., None]
    lse = m[..., 0] + jnp.log(den)
    o = from_strided_blocks(o, s)
    lse = from_strided_blocks(jnp.moveaxis(lse, -1, -2), s)
    return o, lse


def dilated_attention_mixture(q, k, v):
    outs, lses = [], []
    for g, (window, dil) in enumerate(A_DILATION_PAIRS):
        o, l = dilated_window_branch(q[:, :, g], k[:, :, g], v[:, :, g], window, dil)
        outs.append(o)
        lses.append(l)
    wts = jax.nn.softmax(jnp.stack(lses), axis=0)
    return jnp.sum(wts[..., None] * jnp.stack(outs), axis=0)


def conformer_conv(u, conv_w, conv_b, ln_g, ln_b):
    a, gate = jnp.split(u, 2, axis=-1)
    h = a * jax.nn.sigmoid(gate)
    h = causal_depthwise_conv(h, conv_w) + conv_b.astype(h.dtype)
    return jax.nn.silu(layer_norm(h, ln_g, ln_b))


def even_mixer(x, w_in, w_out, conv_w, conv_b, ln_g, ln_b, pos):
    b, s, _ = x.shape
    z = x @ w_in
    q, k, v, u = jnp.split(z, [A_QK_WIDTH, 2 * A_QK_WIDTH, 3 * A_QK_WIDTH], axis=-1)
    hshape = (b, s, A_GROUPS, A_HEADS, A_HEAD_DIM)
    q = rope(q.reshape(hshape), pos)
    k = rope(k.reshape(hshape), pos)
    attn = dilated_attention_mixture(q, k, v.reshape(hshape)).reshape(b, s, A_OUT_WIDTH).astype(x.dtype)
    conv = conformer_conv(u, conv_w, conv_b, ln_g, ln_b)
    return jnp.concatenate([attn, conv], axis=-1) @ w_out


def short_conv_mixer(x, w_in, conv_w, w_out):
    bg, cg, h = jnp.split(x @ w_in, 3, axis=-1)
    return (bg * causal_depthwise_conv(cg * h, conv_w)) @ w_out


def swiglu(x, w_in, w_out):
    gate, up = jnp.split(x @ w_in, 2, axis=-1)
    return (jax.nn.silu(gate) * up) @ w_out


def setup_inputs(seed: int = 0) -> dict:
    key = jax.random.key(seed)
    ks = jax.random.split(key, 19)
    d = D_MODEL

    def nrm(k, shape, scale):
        return jax.random.normal(k, shape, jnp.float32) * scale

    return {
        "x": nrm(ks[0], (BATCH, SEQ, d), 1.0),
        "p": nrm(ks[1], (DEPTH, BATCH, SEQ, PLE_DIM), 1.0),
        "even_w_in": nrm(ks[2], (N_EVEN, d, EVEN_IN_WIDTH), d ** -0.5),
        "even_w_out": nrm(ks[3], (N_EVEN, EVEN_OUT_WIDTH, d), EVEN_OUT_WIDTH ** -0.5 * DN_BETA),
        "conf_conv_w": nrm(ks[4], (N_EVEN, B_KERNEL, B_WIDTH), B_KERNEL ** -0.5),
        "conf_conv_b": nrm(ks[5], (N_EVEN, B_WIDTH), 0.02),
        "conf_ln_g": 1.0 + nrm(ks[6], (N_EVEN, B_WIDTH), 0.02),
        "conf_ln_b": nrm(ks[7], (N_EVEN, B_WIDTH), 0.02),
        "odd_w_in": nrm(ks[8], (N_ODD, d, 3 * C_WIDTH), d ** -0.5),
        "odd_conv_w": nrm(ks[9], (N_ODD, C_KERNEL, C_WIDTH), C_KERNEL ** -0.5),
        "odd_w_out": nrm(ks[10], (N_ODD, C_WIDTH, d), C_WIDTH ** -0.5 * DN_BETA),
        "ln_mix_g": 1.0 + nrm(ks[11], (DEPTH, d), 0.02),
        "ln_mix_b": nrm(ks[12], (DEPTH, d), 0.02),
        "ln_ffn_g": 1.0 + nrm(ks[13], (DEPTH, d), 0.02),
        "ln_ffn_b": nrm(ks[14], (DEPTH, d), 0.02),
        "ffn_w_in": nrm(ks[15], (DEPTH, d, 2 * FFN_HIDDEN), d ** -0.5),
        "ffn_w_out": nrm(ks[16], (DEPTH, FFN_HIDDEN, d), FFN_HIDDEN ** -0.5 * DN_BETA),
        "ple_w_proj": nrm(ks[17], (DEPTH, PLE_DIM, d), PLE_DIM ** -0.5),
        "ple_w_gate": nrm(ks[18], (DEPTH, d, d), d ** -0.5),
    }


def reference(x, p, even_w_in, even_w_out, conf_conv_w, conf_conv_b, conf_ln_g, conf_ln_b,
              odd_w_in, odd_conv_w, odd_w_out, ln_mix_g, ln_mix_b, ln_ffn_g, ln_ffn_b,
              ffn_w_in, ffn_w_out, ple_w_proj, ple_w_gate):
    pos = jnp.arange(x.shape[1], dtype=jnp.int32)
    for i in range(DEPTH):
        j = i // 2
        if i % 2 == 0:
            mix = even_mixer(x, even_w_in[j], even_w_out[j], conf_conv_w[j], conf_conv_b[j],
                             conf_ln_g[j], conf_ln_b[j], pos)
        else:
            mix = short_conv_mixer(x, odd_w_in[j], odd_conv_w[j], odd_w_out[j])
        x = layer_norm(DN_ALPHA * x + mix, ln_mix_g[i], ln_mix_b[i])
        x = layer_norm(DN_ALPHA * x + swiglu(x, ffn_w_in[i], ffn_w_out[i]), ln_ffn_g[i], ln_ffn_b[i])
        x = x + (p[i] @ ple_w_proj[i]) * jax.nn.sigmoid(x @ ple_w_gate[i])
    return x
```

```python
import functools

import jax
import jax.numpy as jnp
from jax import lax
from jax.experimental import pallas as pl
from jax.experimental.pallas import tpu as pltpu

F32 = jnp.float32
BF16 = jnp.bfloat16

D_MODEL = 1024
HEADS = 4
HEAD_DIM = 128
DILATIONS = (1, 4, 16)
ATT_BLOCK = 128
GROUP_WIDTH = HEADS * HEAD_DIM
QK_WIDTH = len(DILATIONS) * GROUP_WIDTH
CONF_WIDTH = 512
CONF_TAPS = 31
SHORT_TAPS = 3
FFN_HIDDEN = 2816
PLE_DIM = 256
ROPE_THETA = 10000.0
NEG_INF = -1e30
LN_EPS = 1e-5
DN_ALPHA = float(4 ** 0.25)

ROW_TILE = 512
FFN_CHUNK = 256
CONV_PAD = 32
CONV_ROWS = 64
SUBLANES = 8
CONV_TAIL = 16
V7X_VMEM_LIMIT = 56 * 1024 * 1024


def _dot(a, b):
    return jnp.dot(a, b, preferred_element_type=F32)


def _layer_norm(v, g, b):
    mu = jnp.mean(v, axis=-1, keepdims=True)
    c = v - mu
    var = jnp.mean(c * c, axis=-1, keepdims=True)
    return c * lax.rsqrt(var + LN_EPS) * g + b


def _sigmoid(v):
    return 1.0 / (1.0 + jnp.exp(-v))


def _resident(shape):
    return pl.BlockSpec(shape, lambda *_: (0,) * len(shape), pipeline_mode=pl.Buffered(1))


def _even_in_kernel(x_ref, w_ref, rope_ref, o0_ref, o1_ref, o2_ref, glu_ref, slab_ref):
    tm = x_ref.shape[1]
    xb = x_ref[0].astype(BF16)
    outs = (o0_ref, o1_ref, o2_ref)
    for j in range(3):
        for g, dil in enumerate(DILATIONS):
            c0 = j * QK_WIDTH + g * GROUP_WIDTH
            acc = _dot(xb, w_ref[:, c0:c0 + GROUP_WIDTH])
            for h in range(HEADS):
                t = acc[:, h * HEAD_DIM:(h + 1) * HEAD_DIM]
                if j < 2:
                    t = t * rope_ref[2 * j] + pltpu.roll(t, HEAD_DIM // 2, 1) * rope_ref[2 * j + 1]
                if dil == 1:
                    o0_ref[0, j, h] = t.astype(BF16)
                else:
                    slab_ref[h] = t
            if dil > 1:
                for h in range(HEADS):
                    for r in range(dil):
                        outs[g][0, j, h, r] = slab_ref[h, pl.ds(r, tm // dil, stride=dil), :].astype(BF16)
    a = _dot(xb, w_ref[:, 3 * QK_WIDTH:3 * QK_WIDTH + CONF_WIDTH])
    gate = _dot(xb, w_ref[:, 3 * QK_WIDTH + CONF_WIDTH:])
    glu_ref[0] = a * _sigmoid(gate)


def _even_in(x, w_in, rope_tab):
    b, s, d = x.shape
    tm = ROW_TILE
    n_cols = w_in.shape[1]
    qkv_shapes = [jax.ShapeDtypeStruct((b, 3, HEADS, s, HEAD_DIM), BF16)]
    qkv_specs = [pl.BlockSpec((1, 3, HEADS, tm, HEAD_DIM), lambda bi, mi: (bi, 0, 0, mi, 0))]
    for dil in DILATIONS[1:]:
        qkv_shapes.append(jax.ShapeDtypeStruct((b, 3, HEADS, dil, s // dil, HEAD_DIM), BF16))
        qkv_specs.append(pl.BlockSpec((1, 3, HEADS, dil, tm // dil, HEAD_DIM),
                                      lambda bi, mi: (bi, 0, 0, 0, mi, 0)))
    return pl.pallas_call(
        _even_in_kernel,
        grid=(b, s // tm),
        in_specs=[
            pl.BlockSpec((1, tm, d), lambda bi, mi: (bi, mi, 0)),
            _resident((d, n_cols)),
            pl.BlockSpec((4, tm, HEAD_DIM), lambda bi, mi: (0, mi, 0)),
        ],
        out_specs=qkv_specs + [pl.BlockSpec((1, tm, CONF_WIDTH), lambda bi, mi: (bi, mi, 0))],
        out_shape=qkv_shapes + [jax.ShapeDtypeStruct((b, s, CONF_WIDTH), F32)],
        scratch_shapes=[pltpu.VMEM((HEADS, tm, HEAD_DIM), F32)],
        compiler_params=pltpu.CompilerParams(
            dimension_semantics=("arbitrary", "arbitrary"), vmem_limit_bytes=V7X_VMEM_LIMIT),
        name="even_in_proj",
    )(x, w_in, rope_tab)


def _attend(q, k, v, mask):
    s = lax.dot_general(q, k, (((1,), (1,)), ((), ())), preferred_element_type=F32)
    s = jnp.where(mask, s, NEG_INF)
    m = jnp.max(s, axis=-1, keepdims=True)
    ex = jnp.exp(s - m)
    den = jnp.sum(ex, axis=-1, keepdims=True)
    acc = _dot(ex.astype(BF16), v)
    return acc, jnp.broadcast_to(m, acc.shape), jnp.broadcast_to(den, acc.shape)


def _attn_kernel(g0_ref, g1_ref, g2_ref, o_ref, acc_ref, max_ref, den_ref):
    seq = o_ref.shape[1]
    nb = ATT_BLOCK
    qi = lax.broadcasted_iota(jnp.int32, (nb, 1), 0)
    kj1 = lax.broadcasted_iota(jnp.int32, (1, nb), 1)
    kj2 = lax.broadcasted_iota(jnp.int32, (1, 2 * nb), 1)
    mask_first = kj1 <= qi
    mask_band = (kj2 >= qi) & (kj2 <= qi + nb)

    def run(g, dil, res, block, rows):
        ref = (g0_ref, g1_ref, g2_ref)[g]

        def part(j, start, size):
            if dil == 1:
                return ref[0, j, 0, pl.ds(start, size), :]
            return ref[0, j, 0, res, pl.ds(start, size), :]

        q = part(0, block * nb, nb)
        if block == 0:
            out = _attend(q, part(1, 0, nb), part(2, 0, nb), mask_first)
        else:
            k0 = (block - 1) * nb
            out = _attend(q, part(1, k0, 2 * nb), part(2, k0, 2 * nb), mask_band)
        for dst, val in zip((acc_ref, max_ref, den_ref), out):
            dst[g, rows, :] = val

    for g, dil in enumerate(DILATIONS):
        sub_len = seq // dil
        for res in range(dil):
            for block in range(sub_len // nb):
                start = block * nb * dil + res
                rows = pl.ds(start, nb) if dil == 1 else pl.ds(start, nb, stride=dil)
                run(g, dil, res, block, rows)

    chunk = 256
    for c in range(seq // chunk):
        rows = pl.ds(c * chunk, chunk)
        m = [max_ref[g, rows, :] for g in range(3)]
        top = jnp.maximum(jnp.maximum(m[0], m[1]), m[2])
        num = jnp.zeros((chunk, HEAD_DIM), F32)
        den = jnp.zeros((chunk, HEAD_DIM), F32)
        for g in range(3):
            a = jnp.exp(m[g] - top)
            num = num + a * acc_ref[g, rows, :]
            den = den + a * den_ref[g, rows, :]
        o_ref[0, rows, :] = (num / den).astype(o_ref.dtype)


def _attention(g0, g1, g2):
    b, _, _, s, e = g0.shape
    in_specs = [pl.BlockSpec((1, 3, 1, s, e), lambda bi, hi: (bi, 0, hi, 0, 0))]
    for arr in (g1, g2):
        dil, sub = arr.shape[3], arr.shape[4]
        in_specs.append(pl.BlockSpec((1, 3, 1, dil, sub, e), lambda bi, hi: (bi, 0, hi, 0, 0, 0)))
    return pl.pallas_call(
        _attn_kernel,
        grid=(b, HEADS),
        in_specs=in_specs,
        out_specs=pl.BlockSpec((1, s, e), lambda bi, hi: (bi, 0, hi)),
        out_shape=jax.ShapeDtypeStruct((b, s, GROUP_WIDTH), BF16),
        scratch_shapes=[pltpu.VMEM((3, s, e), F32)] * 3,
        compiler_params=pltpu.CompilerParams(
            dimension_semantics=("arbitrary", "arbitrary"), vmem_limit_bytes=V7X_VMEM_LIMIT),
        name="dilated_attention",
    )(g0, g1, g2)


def _conformer_kernel(h_ref, w_ref, cb_ref, g_ref, b_ref, o_ref, pad_ref):
    seq = h_ref.shape[1]
    pad_ref[0:CONV_PAD, :] = jnp.zeros((CONV_PAD, CONF_WIDTH), F32)
    pad_ref[CONV_PAD:CONV_PAD + seq, :] = h_ref[0]
    pad_ref[CONV_PAD + seq:, :] = jnp.zeros((CONV_TAIL, CONF_WIDTH), F32)
    first = CONV_PAD - (CONF_TAPS - 1)

    def step(i, carry):
        t0 = pl.multiple_of(i * CONV_ROWS, CONV_ROWS)
        acc = jnp.broadcast_to(cb_ref[...], (CONV_ROWS, CONF_WIDTH))
        for s in range(SUBLANES):
            part = None
            for j in range(s, CONF_TAPS, SUBLANES):
                term = w_ref[j:j + 1, :] * pad_ref[pl.ds(t0 + (j - s), CONV_ROWS + CONV_TAIL), :]
                part = term if part is None else part + term
            acc = acc + part[first + s:first + s + CONV_ROWS]
        y = _layer_norm(acc, g_ref[...], b_ref[...])
        o_ref[0, pl.ds(t0, CONV_ROWS), :] = (y * _sigmoid(y)).astype(o_ref.dtype)
        return carry

    lax.fori_loop(0, seq // CONV_ROWS, step, 0)


def _conformer(glu, conv_w, conv_b, ln_g, ln_b):
    b, s, c = glu.shape
    row = lambda v: v.reshape(1, c)
    return pl.pallas_call(
        _conformer_kernel,
        grid=(b,),
        in_specs=[
            pl.BlockSpec((1, s, c), lambda bi: (bi, 0, 0)),
            _resident((CONF_TAPS, c)),
            _resident((1, c)), _resident((1, c)), _resident((1, c)),
        ],
        out_specs=pl.BlockSpec((1, s, c), lambda bi: (bi, 0, 0)),
        out_shape=jax.ShapeDtypeStruct((b, s, c), BF16),
        scratch_shapes=[pltpu.VMEM((CONV_PAD + s + CONV_TAIL, c), F32)],
        compiler_params=pltpu.CompilerParams(
            dimension_semantics=("arbitrary",), vmem_limit_bytes=V7X_VMEM_LIMIT),
        name="conformer_conv",
    )(glu, conv_w, row(conv_b), row(ln_g), row(ln_b))


def _even_out_kernel(attn_ref, conv_ref, x_ref, w_ref, g_ref, b_ref, o_ref):
    mix = _dot(attn_ref[...], w_ref[0:GROUP_WIDTH, :]) + _dot(conv_ref[...], w_ref[GROUP_WIDTH:, :])
    o_ref[...] = _layer_norm(DN_ALPHA * x_ref[...] + mix, g_ref[...], b_ref[...])


def _even_out(attn, conv, x2d, w_out, ln_g, ln_b):
    n, d = x2d.shape
    tm = ROW_TILE
    return pl.pallas_call(
        _even_out_kernel,
        grid=(n // tm,),
        in_specs=[
            pl.BlockSpec((tm, GROUP_WIDTH), lambda i: (i, 0)),
            pl.BlockSpec((tm, CONF_WIDTH), lambda i: (i, 0)),
            pl.BlockSpec((tm, d), lambda i: (i, 0)),
            _resident(w_out.shape), _resident((1, d)), _resident((1, d)),
        ],
        out_specs=pl.BlockSpec((tm, d), lambda i: (i, 0)),
        out_shape=jax.ShapeDtypeStruct((n, d), F32),
        compiler_params=pltpu.CompilerParams(
            dimension_semantics=("arbitrary",), vmem_limit_bytes=V7X_VMEM_LIMIT),
        name="even_out_proj_ln",
    )(attn, conv, x2d, w_out, ln_g.reshape(1, d), ln_b.reshape(1, d))


SHORT_CARRY = 8


def _odd_mixer_kernel(x_ref, w_in_ref, cw_ref, w_out_ref, g_ref, b_ref, o_ref, gate_ref, mix_ref):
    tm = x_ref.shape[1]
    width = D_MODEL
    chunk = 512

    @pl.when(pl.program_id(1) == 0)
    def _():
        gate_ref[0:SHORT_CARRY, :] = jnp.zeros((SHORT_CARRY, width), F32)

    xb = x_ref[0].astype(BF16)
    for c in range(width // chunk):
        cols = slice(c * chunk, (c + 1) * chunk)
        cg = _dot(xb, w_in_ref[:, width + c * chunk:width + (c + 1) * chunk])
        hh = _dot(xb, w_in_ref[:, 2 * width + c * chunk:2 * width + (c + 1) * chunk])
        gate_ref[SHORT_CARRY:, cols] = cg * hh
        y = jnp.zeros((tm, chunk), F32)
        for j in range(SHORT_TAPS):
            off = SHORT_CARRY - (SHORT_TAPS - 1) + j
            y = y + cw_ref[j:j + 1, cols] * gate_ref[off:off + tm, cols]
        bg = _dot(xb, w_in_ref[:, cols])
        mix_ref[:, cols] = (bg * y).astype(BF16)
        gate_ref[0:SHORT_CARRY, cols] = gate_ref[tm:tm + SHORT_CARRY, cols]
    mix = _dot(mix_ref[...], w_out_ref[...])
    o_ref[0] = _layer_norm(DN_ALPHA * x_ref[0] + mix, g_ref[...], b_ref[...])


def _odd_mixer(x, w_in, conv_w, w_out, ln_g, ln_b):
    b, s, d = x.shape
    tm = ROW_TILE
    return pl.pallas_call(
        _odd_mixer_kernel,
        grid=(b, s // tm),
        in_specs=[
            pl.BlockSpec((1, tm, d), lambda bi, mi: (bi, mi, 0)),
            _resident(w_in.shape), _resident(conv_w.shape), _resident(w_out.shape),
            _resident((1, d)), _resident((1, d)),
        ],
        out_specs=pl.BlockSpec((1, tm, d), lambda bi, mi: (bi, mi, 0)),
        out_shape=jax.ShapeDtypeStruct((b, s, d), F32),
        scratch_shapes=[pltpu.VMEM((SHORT_CARRY + tm, d), F32), pltpu.VMEM((tm, d), BF16)],
        compiler_params=pltpu.CompilerParams(
            dimension_semantics=("arbitrary", "arbitrary"), vmem_limit_bytes=V7X_VMEM_LIMIT),
        name="short_conv_mixer_ln",
    )(x, w_in, conv_w, w_out, ln_g.reshape(1, d), ln_b.reshape(1, d))


def _ffn_kernel(x_ref, p_ref, w_in_ref, w_out_ref, g_ref, b_ref, wp_ref, wg_ref, o_ref, hid_ref):
    x = x_ref[...]
    xb = x.astype(BF16)
    for c in range(FFN_HIDDEN // FFN_CHUNK):
        cols = slice(c * FFN_CHUNK, (c + 1) * FFN_CHUNK)
        gate = _dot(xb, w_in_ref[:, cols])
        up = _dot(xb, w_in_ref[:, FFN_HIDDEN + c * FFN_CHUNK:FFN_HIDDEN + (c + 1) * FFN_CHUNK])
        hid_ref[:, cols] = (gate * _sigmoid(gate) * up).astype(BF16)
    y = _dot(hid_ref[...], w_out_ref[...])
    x2 = _layer_norm(DN_ALPHA * x + y, g_ref[...], b_ref[...])
    emb = _dot(p_ref[...].astype(BF16), wp_ref[...])
    o_ref[...] = x2 + emb * _sigmoid(_dot(x2.astype(BF16), wg_ref[...]))


def _ffn(x2d, p2d, w_in, w_out, ln_g, ln_b, w_proj, w_gate):
    n, d = x2d.shape
    tm = ROW_TILE
    return pl.pallas_call(
        _ffn_kernel,
        grid=(n // tm,),
        in_specs=[
            pl.BlockSpec((tm, d), lambda i: (i, 0)),
            pl.BlockSpec((tm, PLE_DIM), lambda i: (i, 0)),
            _resident(w_in.shape), _resident(w_out.shape),
            _resident((1, d)), _resident((1, d)),
            _resident(w_proj.shape), _resident(w_gate.shape),
        ],
        out_specs=pl.BlockSpec((tm, d), lambda i: (i, 0)),
        out_shape=jax.ShapeDtypeStruct((n, d), F32),
        scratch_shapes=[pltpu.VMEM((tm, FFN_HIDDEN), BF16)],
        compiler_params=pltpu.CompilerParams(
            dimension_semantics=("arbitrary",), vmem_limit_bytes=V7X_VMEM_LIMIT),
        name="swiglu_ln_ple",
    )(x2d, p2d, w_in, w_out, ln_g.reshape(1, d), ln_b.reshape(1, d), w_proj, w_gate)


def _rope_tables(seq):
    half = HEAD_DIM // 2
    inv = ROPE_THETA ** (-jnp.arange(half, dtype=F32) / half)
    ang = jnp.arange(seq, dtype=F32)[:, None] * inv[None, :]
    cos = jnp.concatenate([jnp.cos(ang), jnp.cos(ang)], axis=-1)
    sin = jnp.concatenate([-jnp.sin(ang), jnp.sin(ang)], axis=-1)
    scale = HEAD_DIM ** -0.5
    return jnp.stack([cos * scale, sin * scale, cos, sin])


def kernel(x, p, even_w_in, even_w_out, conf_conv_w, conf_conv_b, conf_ln_g, conf_ln_b, odd_w_in, odd_conv_w, odd_w_out, ln_mix_g, ln_mix_b, ln_ffn_g, ln_ffn_b, ffn_w_in, ffn_w_out, ple_w_proj, ple_w_gate):
    b, s, d = x.shape
    depth = p.shape[0]
    rope_tab = _rope_tables(s)
    for i in range(depth):
        j = i // 2
        if i % 2 == 0:
            g0, g1, g2, glu = _even_in(x, even_w_in[j].astype(BF16), rope_tab)
            attn = _attention(g0, g1, g2)
            conv = _conformer(glu, conf_conv_w[j], conf_conv_b[j], conf_ln_g[j], conf_ln_b[j])
            x1 = _even_out(attn.reshape(b * s, -1), conv.reshape(b * s, -1), x.reshape(b * s, d),
                           even_w_out[j].astype(BF16), ln_mix_g[i], ln_mix_b[i])
        else:
            x1 = _odd_mixer(x, odd_w_in[j].astype(BF16), odd_conv_w[j], odd_w_out[j].astype(BF16),
                            ln_mix_g[i], ln_mix_b[i]).reshape(b * s, d)
        x = _ffn(x1, p[i].reshape(b * s, -1), ffn_w_in[i].astype(BF16), ffn_w_out[i].astype(BF16),
                 ln_ffn_g[i], ln_ffn_b[i], ple_w_proj[i].astype(BF16), ple_w_gate[i].astype(BF16))
        x = x.reshape(b, s, d)
    return x
```

```python
import jax
import jax.numpy as jnp
import numpy as np
from jax import lax
from jax.experimental import pallas as pl
from jax.experimental.pallas import tpu as pltpu

F32 = jnp.float32
BF16 = jnp.bfloat16

D_MODEL = 1024
HEADS = 4
HEAD_DIM = 128
DILATIONS = (1, 4, 16)
ATT_BLOCK = 128
GROUP_WIDTH = HEADS * HEAD_DIM
QK_WIDTH = len(DILATIONS) * GROUP_WIDTH
CONF_WIDTH = 512
CONF_TAPS = 31
SHORT_TAPS = 3
FFN_HIDDEN = 2816
PLE_DIM = 256
ROPE_THETA = 10000.0
NEG_INF = -1e30
LN_EPS = 1e-5
DN_ALPHA = float(4 ** 0.25)

ROW_TILE = 512
FFN_CHUNK = 256
CONV_PAD = 32
CONV_ROWS = 64
SUBLANES = 8
CONV_TAIL = 16
V7X_VMEM_LIMIT = 56 * 1024 * 1024


def _dot(a, b):
    return jnp.dot(a, b, preferred_element_type=F32)


def _layer_norm(v, g, b):
    mu = jnp.mean(v, axis=-1, keepdims=True)
    c = v - mu
    var = jnp.mean(c * c, axis=-1, keepdims=True)
    return c * lax.rsqrt(var + LN_EPS) * g + b


def _sigmoid(v):
    return 1.0 / (1.0 + jnp.exp(-v))


def _resident(shape):
    return pl.BlockSpec(shape, lambda *_: (0,) * len(shape), pipeline_mode=pl.Buffered(1))


def _layer_slab(stacked, layer):
    return pl.BlockSpec((None,) + stacked.shape[1:], lambda *_: (layer, 0, 0), pipeline_mode=pl.Buffered(1))


def _rows(stacked):
    return stacked.reshape(stacked.shape[0], 1, stacked.shape[1])


def _even_in_kernel(x_ref, w_ref, rope_ref, o0_ref, o1_ref, o2_ref, glu_ref, slab_ref):
    tm = x_ref.shape[1]
    xb = x_ref[0].astype(BF16)
    outs = (o0_ref, o1_ref, o2_ref)
    for j in range(3):
        for g, dil in enumerate(DILATIONS):
            c0 = j * QK_WIDTH + g * GROUP_WIDTH
            acc = _dot(xb, w_ref[:, c0:c0 + GROUP_WIDTH])
            for h in range(HEADS):
                t = acc[:, h * HEAD_DIM:(h + 1) * HEAD_DIM]
                if j < 2:
                    t = t * rope_ref[2 * j] + pltpu.roll(t, HEAD_DIM // 2, 1) * rope_ref[2 * j + 1]
                if dil == 1:
                    o0_ref[0, j, h] = t.astype(BF16)
                else:
                    slab_ref[h] = t
            if dil > 1:
                for h in range(HEADS):
                    for r in range(dil):
                        outs[g][0, j, h, r] = slab_ref[h, pl.ds(r, tm // dil, stride=dil), :].astype(BF16)
    a = _dot(xb, w_ref[:, 3 * QK_WIDTH:3 * QK_WIDTH + CONF_WIDTH])
    gate = _dot(xb, w_ref[:, 3 * QK_WIDTH + CONF_WIDTH:])
    glu_ref[0] = a * _sigmoid(gate)


def _even_in(x, w_in, layer, rope_tab):
    b, s, d = x.shape
    tm = ROW_TILE
    qkv_shapes = [jax.ShapeDtypeStruct((b, 3, HEADS, s, HEAD_DIM), BF16)]
    qkv_specs = [pl.BlockSpec((1, 3, HEADS, tm, HEAD_DIM), lambda bi, mi: (bi, 0, 0, mi, 0))]
    for dil in DILATIONS[1:]:
        qkv_shapes.append(jax.ShapeDtypeStruct((b, 3, HEADS, dil, s // dil, HEAD_DIM), BF16))
        qkv_specs.append(pl.BlockSpec((1, 3, HEADS, dil, tm // dil, HEAD_DIM),
                                      lambda bi, mi: (bi, 0, 0, 0, mi, 0)))
    return pl.pallas_call(
        _even_in_kernel,
        grid=(b, s // tm),
        in_specs=[
            pl.BlockSpec((1, tm, d), lambda bi, mi: (bi, mi, 0)),
            _layer_slab(w_in, layer),
            pl.BlockSpec((4, tm, HEAD_DIM), lambda bi, mi: (0, mi, 0)),
        ],
        out_specs=qkv_specs + [pl.BlockSpec((1, tm, CONF_WIDTH), lambda bi, mi: (bi, mi, 0))],
        out_shape=qkv_shapes + [jax.ShapeDtypeStruct((b, s, CONF_WIDTH), F32)],
        scratch_shapes=[pltpu.VMEM((HEADS, tm, HEAD_DIM), F32)],
        compiler_params=pltpu.CompilerParams(
            dimension_semantics=("arbitrary", "arbitrary"), vmem_limit_bytes=V7X_VMEM_LIMIT),
        name="even_in_proj",
    )(x, w_in, rope_tab)


ATT_BATCH = 8


def _attn_kernel(g0_ref, g1_ref, g2_ref, o_ref, kk_ref, vv_ref, out_ref, lse_ref):
    seq = o_ref.shape[1]
    nb = ATT_BLOCK
    n_blocks = seq // nb
    qi = lax.broadcasted_iota(jnp.int32, (nb, 1), 0)
    kj = lax.broadcasted_iota(jnp.int32, (1, 2 * nb), 1)
    cur_only = (kj >= nb) & (kj <= qi + nb)
    band = (kj >= qi) & (kj <= qi + nb)

    for g, ref, dil in zip(range(3), (g0_ref, g1_ref, g2_ref), DILATIONS):
        per_res = n_blocks // dil
        firsts = range(0, n_blocks, per_res)
        if per_res > 1:
            for j, win_ref in ((1, kk_ref), (2, vv_ref)):
                win_ref[:, nb:, :] = ref[0, j, 0]
                win_ref[1:, 0:nb, :] = ref[0, j, 0, 0:n_blocks - 1]
                for i in firsts:
                    win_ref[i, 0:nb, :] = jnp.zeros((nb, HEAD_DIM), BF16)
            keys, vals, lo = (lambda i: kk_ref[i]), (lambda i: vv_ref[i]), 0
        else:
            keys, vals, lo = (lambda i: ref[0, 1, 0, i]), (lambda i: ref[0, 2, 0, i]), nb

        for c0 in range(0, n_blocks, ATT_BATCH):
            blocks = list(range(c0, c0 + ATT_BATCH))
            s = [lax.dot_general(ref[0, 0, 0, i], keys(i), (((1,), (1,)), ((), ())),
                                 preferred_element_type=F32) for i in blocks]
            masks = [(cur_only if i in firsts else band)[:, lo:] for i in blocks]
            s = [jnp.where(mk, v, NEG_INF) for mk, v in zip(masks, s)]
            m = [jnp.max(v, axis=-1, keepdims=True) for v in s]
            ex = [jnp.exp2(v - mx) for v, mx in zip(s, m)]
            dens = [jnp.sum(e, axis=-1, keepdims=True) for e in ex]
            acc = [_dot(e.astype(BF16), vals(i)) for e, i in zip(ex, blocks)]
            for i, a, mx, den in zip(blocks, acc, m, dens):
                start = (i % per_res) * nb * dil + i // per_res
                rows = pl.ds(start, nb) if dil == 1 else pl.ds(start, nb, stride=dil)
                out_ref[g, rows, :] = a[:, 0:nb] / den
                lse_ref[g, rows, :] = jnp.broadcast_to(mx + jnp.log2(den), (nb, HEAD_DIM))

    chunk = 256
    for c in range(seq // chunk):
        rows = pl.ds(c * chunk, chunk)
        lse = [lse_ref[g, rows, :] for g in range(3)]
        top = jnp.maximum(jnp.maximum(lse[0], lse[1]), lse[2])
        num = jnp.zeros((chunk, HEAD_DIM), F32)
        den = jnp.zeros((chunk, HEAD_DIM), F32)
        for g in range(3):
            a = jnp.exp2(lse[g] - top)
            num = num + a * out_ref[g, rows, :]
            den = den + a
        o_ref[0, rows, :] = (num / den).astype(o_ref.dtype)


def _attention(g0, g1, g2):
    b, _, _, s, e = g0.shape
    nb = ATT_BLOCK
    blocked = [a.reshape(b, 3, HEADS, s // nb, nb, e) for a in (g0, g1, g2)]
    spec = pl.BlockSpec((1, 3, 1, s // nb, nb, e), lambda bi, hi: (bi, 0, hi, 0, 0, 0))
    return pl.pallas_call(
        _attn_kernel,
        grid=(b, HEADS),
        in_specs=[spec] * 3,
        out_specs=pl.BlockSpec((1, s, e), lambda bi, hi: (bi, 0, hi)),
        out_shape=jax.ShapeDtypeStruct((b, s, GROUP_WIDTH), BF16),
        scratch_shapes=[pltpu.VMEM((s // nb, 2 * nb, e), BF16)] * 2 + [pltpu.VMEM((3, s, e), F32)] * 2,
        compiler_params=pltpu.CompilerParams(
            dimension_semantics=("arbitrary", "arbitrary"), vmem_limit_bytes=V7X_VMEM_LIMIT),
        name="dilated_attention",
    )(*blocked)


def _conformer_kernel(h_ref, w_ref, cb_ref, g_ref, b_ref, o_ref, pad_ref):
    seq = h_ref.shape[1]
    pad_ref[0:CONV_PAD, :] = jnp.zeros((CONV_PAD, CONF_WIDTH), F32)
    pad_ref[CONV_PAD:CONV_PAD + seq, :] = h_ref[0]
    pad_ref[CONV_PAD + seq:, :] = jnp.zeros((CONV_TAIL, CONF_WIDTH), F32)
    first = CONV_PAD - (CONF_TAPS - 1)

    def step(i, carry):
        t0 = pl.multiple_of(i * CONV_ROWS, CONV_ROWS)
        acc = jnp.broadcast_to(cb_ref[...], (CONV_ROWS, CONF_WIDTH))
        for s in range(SUBLANES):
            part = None
            for j in range(s, CONF_TAPS, SUBLANES):
                term = w_ref[j:j + 1, :] * pad_ref[pl.ds(t0 + (j - s), CONV_ROWS + CONV_TAIL), :]
                part = term if part is None else part + term
            acc = acc + part[first + s:first + s + CONV_ROWS]
        y = _layer_norm(acc, g_ref[...], b_ref[...])
        o_ref[0, pl.ds(t0, CONV_ROWS), :] = (y * _sigmoid(y)).astype(o_ref.dtype)
        return carry

    lax.fori_loop(0, seq // CONV_ROWS, step, 0)


def _conformer(glu, conv_w, conv_b, ln_g, ln_b, layer):
    b, s, c = glu.shape
    return pl.pallas_call(
        _conformer_kernel,
        grid=(b,),
        in_specs=[
            pl.BlockSpec((1, s, c), lambda bi: (bi, 0, 0)),
            _layer_slab(conv_w, layer),
            _layer_slab(conv_b, layer), _layer_slab(ln_g, layer), _layer_slab(ln_b, layer),
        ],
        out_specs=pl.BlockSpec((1, s, c), lambda bi: (bi, 0, 0)),
        out_shape=jax.ShapeDtypeStruct((b, s, c), BF16),
        scratch_shapes=[pltpu.VMEM((CONV_PAD + s + CONV_TAIL, c), F32)],
        compiler_params=pltpu.CompilerParams(
            dimension_semantics=("arbitrary",), vmem_limit_bytes=V7X_VMEM_LIMIT),
        name="conformer_conv",
    )(glu, conv_w, conv_b, ln_g, ln_b)


def _even_out_kernel(attn_ref, conv_ref, x_ref, w_ref, g_ref, b_ref, o_ref):
    mix = _dot(attn_ref[...], w_ref[0:GROUP_WIDTH, :]) + _dot(conv_ref[...], w_ref[GROUP_WIDTH:, :])
    o_ref[...] = _layer_norm(DN_ALPHA * x_ref[...] + mix, g_ref[...], b_ref[...])


def _even_out(attn, conv, x2d, w_out, layer, ln_g, ln_b, ln_layer):
    n, d = x2d.shape
    tm = ROW_TILE
    return pl.pallas_call(
        _even_out_kernel,
        grid=(n // tm,),
        in_specs=[
            pl.BlockSpec((tm, GROUP_WIDTH), lambda i: (i, 0)),
            pl.BlockSpec((tm, CONF_WIDTH), lambda i: (i, 0)),
            pl.BlockSpec((tm, d), lambda i: (i, 0)),
            _layer_slab(w_out, layer), _layer_slab(ln_g, ln_layer), _layer_slab(ln_b, ln_layer),
        ],
        out_specs=pl.BlockSpec((tm, d), lambda i: (i, 0)),
        out_shape=jax.ShapeDtypeStruct((n, d), F32),
        compiler_params=pltpu.CompilerParams(
            dimension_semantics=("arbitrary",), vmem_limit_bytes=V7X_VMEM_LIMIT),
        name="even_out_proj_ln",
    )(attn, conv, x2d, w_out, ln_g, ln_b)


SHORT_CARRY = 8


def _odd_mixer_kernel(x_ref, w_in_ref, cw_ref, w_out_ref, g_ref, b_ref, o_ref, gate_ref, mix_ref):
    tm = x_ref.shape[1]
    width = D_MODEL
    chunk = 512

    @pl.when(pl.program_id(1) == 0)
    def _():
        gate_ref[0:SHORT_CARRY, :] = jnp.zeros((SHORT_CARRY, width), F32)

    xb = x_ref[0].astype(BF16)
    for c in range(width // chunk):
        cols = slice(c * chunk, (c + 1) * chunk)
        cg = _dot(xb, w_in_ref[:, width + c * chunk:width + (c + 1) * chunk])
        hh = _dot(xb, w_in_ref[:, 2 * width + c * chunk:2 * width + (c + 1) * chunk])
        gate_ref[SHORT_CARRY:, cols] = cg * hh
        y = jnp.zeros((tm, chunk), F32)
        for j in range(SHORT_TAPS):
            off = SHORT_CARRY - (SHORT_TAPS - 1) + j
            y = y + cw_ref[j:j + 1, cols] * gate_ref[off:off + tm, cols]
        bg = _dot(xb, w_in_ref[:, cols])
        mix_ref[:, cols] = (bg * y).astype(BF16)
        gate_ref[0:SHORT_CARRY, cols] = gate_ref[tm:tm + SHORT_CARRY, cols]
    mix = _dot(mix_ref[...], w_out_ref[...])
    o_ref[0] = _layer_norm(DN_ALPHA * x_ref[0] + mix, g_ref[...], b_ref[...])


def _odd_mixer(x, w_in, conv_w, w_out, layer, ln_g, ln_b, ln_layer):
    b, s, d = x.shape
    tm = ROW_TILE
    return pl.pallas_call(
        _odd_mixer_kernel,
        grid=(b, s // tm),
        in_specs=[
            pl.BlockSpec((1, tm, d), lambda bi, mi: (bi, mi, 0)),
            _layer_slab(w_in, layer), _layer_slab(conv_w, layer), _layer_slab(w_out, layer),
            _layer_slab(ln_g, ln_layer), _layer_slab(ln_b, ln_layer),
        ],
        out_specs=pl.BlockSpec((1, tm, d), lambda bi, mi: (bi, mi, 0)),
        out_shape=jax.ShapeDtypeStruct((b, s, d), F32),
        scratch_shapes=[pltpu.VMEM((SHORT_CARRY + tm, d), F32), pltpu.VMEM((tm, d), BF16)],
        compiler_params=pltpu.CompilerParams(
            dimension_semantics=("arbitrary", "arbitrary"), vmem_limit_bytes=V7X_VMEM_LIMIT),
        name="short_conv_mixer_ln",
    )(x, w_in, conv_w, w_out, ln_g, ln_b)


def _ffn_kernel(x_ref, p_ref, w_in_ref, w_out_ref, g_ref, b_ref, wp_ref, wg_ref, o_ref, hid_ref):
    x = x_ref[...]
    xb = x.astype(BF16)
    for c in range(FFN_HIDDEN // FFN_CHUNK):
        cols = slice(c * FFN_CHUNK, (c + 1) * FFN_CHUNK)
        gate = _dot(xb, w_in_ref[:, cols])
        up = _dot(xb, w_in_ref[:, FFN_HIDDEN + c * FFN_CHUNK:FFN_HIDDEN + (c + 1) * FFN_CHUNK])
        hid_ref[:, cols] = (gate * _sigmoid(gate) * up).astype(BF16)
    y = _dot(hid_ref[...], w_out_ref[...])
    x2 = _layer_norm(DN_ALPHA * x + y, g_ref[...], b_ref[...])
    emb = _dot(p_ref[...].astype(BF16), wp_ref[...])
    o_ref[...] = x2 + emb * _sigmoid(_dot(x2.astype(BF16), wg_ref[...]))


def _ffn(x2d, p_all, w_in, w_out, ln_g, ln_b, w_proj, w_gate, layer):
    n, d = x2d.shape
    tm = ROW_TILE
    slabs = [_layer_slab(a, layer) for a in (w_in, w_out, ln_g, ln_b, w_proj, w_gate)]
    return pl.pallas_call(
        _ffn_kernel,
        grid=(n // tm,),
        in_specs=[
            pl.BlockSpec((tm, d), lambda i: (i, 0)),
            pl.BlockSpec((None, tm, PLE_DIM), lambda i: (layer, i, 0)),
        ] + slabs,
        out_specs=pl.BlockSpec((tm, d), lambda i: (i, 0)),
        out_shape=jax.ShapeDtypeStruct((n, d), F32),
        scratch_shapes=[pltpu.VMEM((tm, FFN_HIDDEN), BF16)],
        compiler_params=pltpu.CompilerParams(
            dimension_semantics=("arbitrary",), vmem_limit_bytes=V7X_VMEM_LIMIT),
        name="swiglu_ln_ple",
    )(x2d, p_all, w_in, w_out, ln_g, ln_b, w_proj, w_gate)


def _rope_tables(seq):
    half = HEAD_DIM // 2
    inv = ROPE_THETA ** (-np.arange(half, dtype=np.float64) / half)
    ang = np.arange(seq, dtype=np.float64)[:, None] * inv[None, :]
    cos = np.concatenate([np.cos(ang), np.cos(ang)], axis=-1)
    sin = np.concatenate([-np.sin(ang), np.sin(ang)], axis=-1)
    scale = HEAD_DIM ** -0.5 * np.log2(np.e)
    return np.stack([cos * scale, sin * scale, cos, sin]).astype(np.float32)


def kernel(x, p, even_w_in, even_w_out, conf_conv_w, conf_conv_b, conf_ln_g, conf_ln_b, odd_w_in, odd_conv_w, odd_w_out, ln_mix_g, ln_mix_b, ln_ffn_g, ln_ffn_b, ffn_w_in, ffn_w_out, ple_w_proj, ple_w_gate):
    b, s, d = x.shape
    depth = p.shape[0]
    rope_tab = _rope_tables(s)
    p_all = p.reshape(depth, b * s, PLE_DIM)
    even_w_in, even_w_out, odd_w_in, odd_w_out, ffn_w_in, ffn_w_out, ple_w_proj, ple_w_gate = (
        w.astype(BF16) for w in
        (even_w_in, even_w_out, odd_w_in, odd_w_out, ffn_w_in, ffn_w_out, ple_w_proj, ple_w_gate))
    conf_conv_b, conf_ln_g, conf_ln_b, ln_mix_g, ln_mix_b, ln_ffn_g, ln_ffn_b = (
        _rows(v) for v in (conf_conv_b, conf_ln_g, conf_ln_b, ln_mix_g, ln_mix_b, ln_ffn_g, ln_ffn_b))
    for i in range(depth):
        j = i // 2
        if i % 2 == 0:
            g0, g1, g2, glu = _even_in(x, even_w_in, j, rope_tab)
            attn = _attention(g0, g1, g2)
            conv = _conformer(glu, conf_conv_w, conf_conv_b, conf_ln_g, conf_ln_b, j)
            x1 = _even_out(attn.reshape(b * s, -1), conv.reshape(b * s, -1), x.reshape(b * s, d),
                           even_w_out, j, ln_mix_g, ln_mix_b, i)
        else:
            x1 = _odd_mixer(x, odd_w_in, odd_conv_w, odd_w_out, j, ln_mix_g, ln_mix_b, i).reshape(b * s, d)
        x = _ffn(x1, p_all, ffn_w_in, ffn_w_out, ln_ffn_g, ln_ffn_b, ple_w_proj, ple_w_gate, i)
        x = x.reshape(b, s, d)
    return x
```

```python
import jax
import jax.numpy as jnp
import numpy as np
from jax import lax
from jax.experimental import pallas as pl
from jax.experimental.pallas import tpu as pltpu

F32 = jnp.float32
BF16 = jnp.bfloat16

D_MODEL = 1024
HEADS = 4
HEAD_DIM = 128
DILATIONS = (1, 4, 16)
ATT_BLOCK = 128
GROUP_WIDTH = HEADS * HEAD_DIM
QK_WIDTH = len(DILATIONS) * GROUP_WIDTH
CONF_WIDTH = 512
CONF_TAPS = 31
SHORT_TAPS = 3
FFN_HIDDEN = 2816
PLE_DIM = 256
ROPE_THETA = 10000.0
NEG_INF = -1e30
LN_EPS = 1e-5
DN_ALPHA = float(4 ** 0.25)

ROW_TILE = 512
FFN_CHUNK = 256
CONV_PAD = 32
CONV_ROWS = 64
SUBLANES = 8
LANES = 128
CONV_TAIL = SUBLANES
V7X_VMEM_LIMIT = 56 * 1024 * 1024


def _dot(a, b):
    return jnp.dot(a, b, preferred_element_type=F32)


def _layer_norm(v, g, b):
    mu = jnp.mean(v, axis=-1, keepdims=True)
    c = v - mu
    var = jnp.mean(c * c, axis=-1, keepdims=True)
    return c * lax.rsqrt(var + LN_EPS) * g + b


def _sigmoid(v):
    return 1.0 / (1.0 + jnp.exp(-v))


def _shift_rows_up(v, shift):
    rows = v.shape[0] - SUBLANES
    if shift == 0:
        return v[0:rows]
    tiles = [pltpu.roll(v[k:k + SUBLANES], SUBLANES - shift, 0) for k in range(0, rows + SUBLANES, SUBLANES)]
    low = lax.broadcasted_iota(jnp.int32, (SUBLANES, v.shape[1]), 0) < SUBLANES - shift
    return jnp.concatenate([jnp.where(low, a, b) for a, b in zip(tiles[:-1], tiles[1:])], axis=0)


def _resident(shape):
    return pl.BlockSpec(shape, lambda *_: (0,) * len(shape), pipeline_mode=pl.Buffered(1))


def _layer_slab(stacked, layer):
    return pl.BlockSpec((None,) + stacked.shape[1:], lambda *_: (layer, 0, 0), pipeline_mode=pl.Buffered(1))


def _rows(stacked):
    return stacked.reshape(stacked.shape[0], 1, stacked.shape[1])


def _even_in_kernel(x_ref, w_ref, rope_ref, cw_ref, cb_ref, lg_ref, lb_ref,
                    o0_ref, o1_ref, o2_ref, conv_ref, slab_ref, hist_ref, pre_ref):
    tm = x_ref.shape[1]
    first = CONV_PAD - (CONF_TAPS - 1)

    @pl.when((pl.program_id(0) == 0) & (pl.program_id(1) == 0))
    def _():
        hist_ref[...] = jnp.zeros(hist_ref.shape, F32)

    @pl.when(pl.program_id(1) == 0)
    def _():
        hist_ref[0:CONV_PAD, :] = jnp.zeros((CONV_PAD, CONF_WIDTH), F32)

    xb = x_ref[0].astype(BF16)
    a = _dot(xb, w_ref[:, 3 * QK_WIDTH:3 * QK_WIDTH + CONF_WIDTH])
    gate = _dot(xb, w_ref[:, 3 * QK_WIDTH + CONF_WIDTH:])
    hist_ref[CONV_PAD:CONV_PAD + tm, :] = a * _sigmoid(gate)

    def zero_from(anchor):
        bits = pltpu.bitcast(anchor, jnp.int32)
        return lax.shift_right_logical(lax.shift_right_logical(bits, 16), 16).astype(F32)[0:1, :]

    def conv_block(row_block, lane_block, after):
        lanes = slice(lane_block * LANES, (lane_block + 1) * LANES)
        t0 = row_block * CONV_ROWS
        hold = zero_from(after[0])
        for other in after[1:]:
            hold = hold + zero_from(other)
        win = hist_ref[t0:t0 + CONV_ROWS + CONV_PAD + CONV_TAIL, lanes] + hold
        acc = jnp.broadcast_to(cb_ref[:, lanes], (CONV_ROWS, LANES))
        for s in range(SUBLANES):
            base, shift = divmod(first + s, SUBLANES)
            part = None
            for j in range(s, CONF_TAPS, SUBLANES):
                k0 = (j - s) + base * SUBLANES
                term = cw_ref[j:j + 1, lanes] * win[k0:k0 + CONV_ROWS + SUBLANES]
                part = term if part is None else part + term
            acc = acc + _shift_rows_up(part, shift)
        pre_ref[t0:t0 + CONV_ROWS, lanes] = acc
        return acc[0:SUBLANES]

    conv_blocks = [(rb, lb) for rb in range(tm // CONV_ROWS) for lb in range(CONF_WIDTH // LANES)]
    qkv_dots = [(j, g) for j in range(3) for g in range(len(DILATIONS))]
    per_dot = -(-len(conv_blocks) // len(qkv_dots))
    outs = (o0_ref, o1_ref, o2_ref)
    done = None
    for idx, (j, g) in enumerate(qkv_dots):
        dil = DILATIONS[g]
        c0 = j * QK_WIDTH + g * GROUP_WIDTH
        acc = _dot(xb, w_ref[:, c0:c0 + GROUP_WIDTH])
        for h in range(HEADS):
            t = acc[:, h * HEAD_DIM:(h + 1) * HEAD_DIM]
            if j < 2:
                t = t * rope_ref[2 * j] + pltpu.roll(t, HEAD_DIM // 2, 1) * rope_ref[2 * j + 1]
            if dil == 1:
                o0_ref[0, j, h] = t.astype(BF16)
            else:
                slab_ref[h] = t
        if dil > 1:
            for h in range(HEADS):
                for r in range(dil):
                    outs[g][0, j, h, r] = slab_ref[h, pl.ds(r, tm // dil, stride=dil), :].astype(BF16)
        after = [acc[tm - SUBLANES:, GROUP_WIDTH - LANES:]] + ([done] if done is not None else [])
        for rb, lb in conv_blocks[idx * per_dot:(idx + 1) * per_dot]:
            done = conv_block(rb, lb, after)
            after = [done]
            if lb == CONF_WIDTH // LANES - 1:
                rows = slice(rb * CONV_ROWS, (rb + 1) * CONV_ROWS)
                y = _layer_norm(pre_ref[rows, :], lg_ref[...], lb_ref[...])
                conv_ref[0, rows, :] = (y * _sigmoid(y)).astype(BF16)
    hist_ref[0:CONV_PAD, :] = hist_ref[tm:tm + CONV_PAD, :]


def _even_in(x, w_in, conv_w, conv_b, ln_g, ln_b, layer, rope_tab):
    b, s, d = x.shape
    tm = ROW_TILE
    qkv_shapes = [jax.ShapeDtypeStruct((b, 3, HEADS, s, HEAD_DIM), BF16)]
    qkv_specs = [pl.BlockSpec((1, 3, HEADS, tm, HEAD_DIM), lambda bi, mi: (bi, 0, 0, mi, 0))]
    for dil in DILATIONS[1:]:
        qkv_shapes.append(jax.ShapeDtypeStruct((b, 3, HEADS, dil, s // dil, HEAD_DIM), BF16))
        qkv_specs.append(pl.BlockSpec((1, 3, HEADS, dil, tm // dil, HEAD_DIM),
                                      lambda bi, mi: (bi, 0, 0, 0, mi, 0)))
    return pl.pallas_call(
        _even_in_kernel,
        grid=(b, s // tm),
        in_specs=[
            pl.BlockSpec((1, tm, d), lambda bi, mi: (bi, mi, 0)),
            _layer_slab(w_in, layer),
            pl.BlockSpec((4, tm, HEAD_DIM), lambda bi, mi: (0, mi, 0)),
        ] + [_layer_slab(a, layer) for a in (conv_w, conv_b, ln_g, ln_b)],
        out_specs=qkv_specs + [pl.BlockSpec((1, tm, CONF_WIDTH), lambda bi, mi: (bi, mi, 0))],
        out_shape=qkv_shapes + [jax.ShapeDtypeStruct((b, s, CONF_WIDTH), BF16)],
        scratch_shapes=[pltpu.VMEM((HEADS, tm, HEAD_DIM), F32),
                        pltpu.VMEM((CONV_PAD + tm + CONV_TAIL, CONF_WIDTH), F32),
                        pltpu.VMEM((tm, CONF_WIDTH), F32)],
        compiler_params=pltpu.CompilerParams(
            dimension_semantics=("arbitrary", "arbitrary"), vmem_limit_bytes=V7X_VMEM_LIMIT),
        name="even_in_proj_conformer",
    )(x, w_in, rope_tab, conv_w, conv_b, ln_g, ln_b)


ATT_BATCH = 8


def _attn_kernel(g0_ref, g1_ref, g2_ref, o_ref, kk_ref, vv_ref, out_ref, lse_ref):
    seq = o_ref.shape[1]
    nb = ATT_BLOCK
    n_blocks = seq // nb
    qi = lax.broadcasted_iota(jnp.int32, (nb, 1), 0)
    kj = lax.broadcasted_iota(jnp.int32, (1, 2 * nb), 1)
    cur_only = (kj >= nb) & (kj <= qi + nb)
    band = (kj >= qi) & (kj <= qi + nb)

    for g, ref, dil in zip(range(3), (g0_ref, g1_ref, g2_ref), DILATIONS):
        per_res = n_blocks // dil
        firsts = range(0, n_blocks, per_res)
        if per_res > 1:
            for j, win_ref in ((1, kk_ref), (2, vv_ref)):
                win_ref[:, nb:, :] = ref[0, j, 0]
                win_ref[1:, 0:nb, :] = ref[0, j, 0, 0:n_blocks - 1]
                for i in firsts:
                    win_ref[i, 0:nb, :] = jnp.zeros((nb, HEAD_DIM), BF16)
            keys, vals, lo = (lambda i: kk_ref[i]), (lambda i: vv_ref[i]), 0
        else:
            keys, vals, lo = (lambda i: ref[0, 1, 0, i]), (lambda i: ref[0, 2, 0, i]), nb

        for c0 in range(0, n_blocks, ATT_BATCH):
            blocks = list(range(c0, c0 + ATT_BATCH))
            s = [lax.dot_general(ref[0, 0, 0, i], keys(i), (((1,), (1,)), ((), ())),
                                 preferred_element_type=F32) for i in blocks]
            masks = [(cur_only if i in firsts else band)[:, lo:] for i in blocks]
            s = [jnp.where(mk, v, NEG_INF) for mk, v in zip(masks, s)]
            m = [jnp.max(v, axis=-1, keepdims=True) for v in s]
            ex = [jnp.exp2(v - mx) for v, mx in zip(s, m)]
            dens = [jnp.sum(e, axis=-1, keepdims=True) for e in ex]
            acc = [_dot(e.astype(BF16), vals(i)) for e, i in zip(ex, blocks)]
            for i, a, mx, den in zip(blocks, acc, m, dens):
                start = (i % per_res) * nb * dil + i // per_res
                rows = pl.ds(start, nb) if dil == 1 else pl.ds(start, nb, stride=dil)
                out_ref[g, rows, :] = a[:, 0:nb] / den
                lse_ref[g, rows, :] = jnp.broadcast_to(mx + jnp.log2(den), (nb, HEAD_DIM))

    chunk = 256
    for c in range(seq // chunk):
        rows = pl.ds(c * chunk, chunk)
        lse = [lse_ref[g, rows, :] for g in range(3)]
        top = jnp.maximum(jnp.maximum(lse[0], lse[1]), lse[2])
        num = jnp.zeros((chunk, HEAD_DIM), F32)
        den = jnp.zeros((chunk, HEAD_DIM), F32)
        for g in range(3):
            a = jnp.exp2(lse[g] - top)
            num = num + a * out_ref[g, rows, :]
            den = den + a
        o_ref[0, rows, :] = (num / den).astype(o_ref.dtype)


def _attention(g0, g1, g2):
    b, _, _, s, e = g0.shape
    nb = ATT_BLOCK
    blocked = [a.reshape(b, 3, HEADS, s // nb, nb, e) for a in (g0, g1, g2)]
    spec = pl.BlockSpec((1, 3, 1, s // nb, nb, e), lambda bi, hi: (bi, 0, hi, 0, 0, 0))
    return pl.pallas_call(
        _attn_kernel,
        grid=(b, HEADS),
        in_specs=[spec] * 3,
        out_specs=pl.BlockSpec((1, s, e), lambda bi, hi: (bi, 0, hi)),
        out_shape=jax.ShapeDtypeStruct((b, s, GROUP_WIDTH), BF16),
        scratch_shapes=[pltpu.VMEM((s // nb, 2 * nb, e), BF16)] * 2 + [pltpu.VMEM((3, s, e), F32)] * 2,
        compiler_params=pltpu.CompilerParams(
            dimension_semantics=("arbitrary", "arbitrary"), vmem_limit_bytes=V7X_VMEM_LIMIT),
        name="dilated_attention",
    )(*blocked)


def _even_out_kernel(attn_ref, conv_ref, x_ref, w_ref, g_ref, b_ref, o_ref):
    mix = _dot(attn_ref[...], w_ref[0:GROUP_WIDTH, :]) + _dot(conv_ref[...], w_ref[GROUP_WIDTH:, :])
    o_ref[...] = _layer_norm(DN_ALPHA * x_ref[...] + mix, g_ref[...], b_ref[...])


def _even_out(attn, conv, x2d, w_out, layer, ln_g, ln_b, ln_layer):
    n, d = x2d.shape
    tm = ROW_TILE
    return pl.pallas_call(
        _even_out_kernel,
        grid=(n // tm,),
        in_specs=[
            pl.BlockSpec((tm, GROUP_WIDTH), lambda i: (i, 0)),
            pl.BlockSpec((tm, CONF_WIDTH), lambda i: (i, 0)),
            pl.BlockSpec((tm, d), lambda i: (i, 0)),
            _layer_slab(w_out, layer), _layer_slab(ln_g, ln_layer), _layer_slab(ln_b, ln_layer),
        ],
        out_specs=pl.BlockSpec((tm, d), lambda i: (i, 0)),
        out_shape=jax.ShapeDtypeStruct((n, d), F32),
        compiler_params=pltpu.CompilerParams(
            dimension_semantics=("arbitrary",), vmem_limit_bytes=V7X_VMEM_LIMIT),
        name="even_out_proj_ln",
    )(attn, conv, x2d, w_out, ln_g, ln_b)


SHORT_CARRY = 8


def _odd_mixer_kernel(x_ref, w_in_ref, cw_ref, w_out_ref, g_ref, b_ref, o_ref, gate_ref, mix_ref):
    tm = x_ref.shape[1]
    width = D_MODEL
    chunk = 512

    @pl.when(pl.program_id(1) == 0)
    def _():
        gate_ref[0:SHORT_CARRY, :] = jnp.zeros((SHORT_CARRY, width), F32)

    xb = x_ref[0].astype(BF16)
    for c in range(width // chunk):
        cols = slice(c * chunk, (c + 1) * chunk)
        cg = _dot(xb, w_in_ref[:, width + c * chunk:width + (c + 1) * chunk])
        hh = _dot(xb, w_in_ref[:, 2 * width + c * chunk:2 * width + (c + 1) * chunk])
        gate_ref[SHORT_CARRY:, cols] = cg * hh
        y = jnp.zeros((tm, chunk), F32)
        for j in range(SHORT_TAPS):
            off = SHORT_CARRY - (SHORT_TAPS - 1) + j
            y = y + cw_ref[j:j + 1, cols] * gate_ref[off:off + tm, cols]
        bg = _dot(xb, w_in_ref[:, cols])
        mix_ref[:, cols] = (bg * y).astype(BF16)
        gate_ref[0:SHORT_CARRY, cols] = gate_ref[tm:tm + SHORT_CARRY, cols]
    mix = _dot(mix_ref[...], w_out_ref[...])
    o_ref[0] = _layer_norm(DN_ALPHA * x_ref[0] + mix, g_ref[...], b_ref[...])


def _odd_mixer(x, w_in, conv_w, w_out, layer, ln_g, ln_b, ln_layer):
    b, s, d = x.shape
    tm = ROW_TILE
    return pl.pallas_call(
        _odd_mixer_kernel,
        grid=(b, s // tm),
        in_specs=[
            pl.BlockSpec((1, tm, d), lambda bi, mi: (bi, mi, 0)),
            _layer_slab(w_in, layer), _layer_slab(conv_w, layer), _layer_slab(w_out, layer),
            _layer_slab(ln_g, ln_layer), _layer_slab(ln_b, ln_layer),
        ],
        out_specs=pl.BlockSpec((1, tm, d), lambda bi, mi: (bi, mi, 0)),
        out_shape=jax.ShapeDtypeStruct((b, s, d), F32),
        scratch_shapes=[pltpu.VMEM((SHORT_CARRY + tm, d), F32), pltpu.VMEM((tm, d), BF16)],
        compiler_params=pltpu.CompilerParams(
            dimension_semantics=("arbitrary", "arbitrary"), vmem_limit_bytes=V7X_VMEM_LIMIT),
        name="short_conv_mixer_ln",
    )(x, w_in, conv_w, w_out, ln_g, ln_b)


def _ffn_kernel(x_ref, p_ref, w_in_ref, w_out_ref, g_ref, b_ref, wp_ref, wg_ref, o_ref, hid_ref):
    x = x_ref[...]
    xb = x.astype(BF16)
    for c in range(FFN_HIDDEN // FFN_CHUNK):
        cols = slice(c * FFN_CHUNK, (c + 1) * FFN_CHUNK)
        gate = _dot(xb, w_in_ref[:, cols])
        up = _dot(xb, w_in_ref[:, FFN_HIDDEN + c * FFN_CHUNK:FFN_HIDDEN + (c + 1) * FFN_CHUNK])
        hid_ref[:, cols] = (gate * _sigmoid(gate) * up).astype(BF16)
    y = _dot(hid_ref[...], w_out_ref[...])
    x2 = _layer_norm(DN_ALPHA * x + y, g_ref[...], b_ref[...])
    emb = _dot(p_ref[...].astype(BF16), wp_ref[...])
    o_ref[...] = x2 + emb * _sigmoid(_dot(x2.astype(BF16), wg_ref[...]))


def _ffn(x2d, p_all, w_in, w_out, ln_g, ln_b, w_proj, w_gate, layer):
    n, d = x2d.shape
    tm = ROW_TILE
    slabs = [_layer_slab(a, layer) for a in (w_in, w_out, ln_g, ln_b, w_proj, w_gate)]
    return pl.pallas_call(
        _ffn_kernel,
        grid=(n // tm,),
        in_specs=[
            pl.BlockSpec((tm, d), lambda i: (i, 0)),
            pl.BlockSpec((None, tm, PLE_DIM), lambda i: (layer, i, 0)),
        ] + slabs,
        out_specs=pl.BlockSpec((tm, d), lambda i: (i, 0)),
        out_shape=jax.ShapeDtypeStruct((n, d), F32),
        scratch_shapes=[pltpu.VMEM((tm, FFN_HIDDEN), BF16)],
        compiler_params=pltpu.CompilerParams(
            dimension_semantics=("arbitrary",), vmem_limit_bytes=V7X_VMEM_LIMIT),
        name="swiglu_ln_ple",
    )(x2d, p_all, w_in, w_out, ln_g, ln_b, w_proj, w_gate)


def _rope_tables(seq):
    half = HEAD_DIM // 2
    inv = ROPE_THETA ** (-np.arange(half, dtype=np.float64) / half)
    ang = np.arange(seq, dtype=np.float64)[:, None] * inv[None, :]
    cos = np.concatenate([np.cos(ang), np.cos(ang)], axis=-1)
    sin = np.concatenate([-np.sin(ang), np.sin(ang)], axis=-1)
    scale = HEAD_DIM ** -0.5 * np.log2(np.e)
    return np.stack([cos * scale, sin * scale, cos, sin]).astype(np.float32)


def kernel(x, p, even_w_in, even_w_out, conf_conv_w, conf_conv_b, conf_ln_g, conf_ln_b, odd_w_in, odd_conv_w, odd_w_out, ln_mix_g, ln_mix_b, ln_ffn_g, ln_ffn_b, ffn_w_in, ffn_w_out, ple_w_proj, ple_w_gate):
    b, s, d = x.shape
    depth = p.shape[0]
    rope_tab = _rope_tables(s)
    p_all = p.reshape(depth, b * s, PLE_DIM)
    even_w_in, even_w_out, odd_w_in, odd_w_out, ffn_w_in, ffn_w_out, ple_w_proj, ple_w_gate = (
        w.astype(BF16) for w in
        (even_w_in, even_w_out, odd_w_in, odd_w_out, ffn_w_in, ffn_w_out, ple_w_proj, ple_w_gate))
    conf_conv_b, conf_ln_g, conf_ln_b, ln_mix_g, ln_mix_b, ln_ffn_g, ln_ffn_b = (
        _rows(v) for v in (conf_conv_b, conf_ln_g, conf_ln_b, ln_mix_g, ln_mix_b, ln_ffn_g, ln_ffn_b))
    for i in range(depth):
        j = i // 2
        if i % 2 == 0:
            g0, g1, g2, conv = _even_in(x, even_w_in, conf_conv_w, conf_conv_b, conf_ln_g, conf_ln_b, j, rope_tab)
            attn = _attention(g0, g1, g2)
            x1 = _even_out(attn.reshape(b * s, -1), conv.reshape(b * s, -1), x.reshape(b * s, d),
                           even_w_out, j, ln_mix_g, ln_mix_b, i)
        else:
            x1 = _odd_mixer(x, odd_w_in, odd_conv_w, odd_w_out, j, ln_mix_g, ln_mix_b, i).reshape(b * s, d)
        x = _ffn(x1, p_all, ffn_w_in, ffn_w_out, ln_ffn_g, ln_ffn_b, ple_w_proj, ple_w_gate, i)
        x = x.reshape(b, s, d)
    return x
```

```python
import jax
import jax.numpy as jnp
import numpy as np
from jax import lax
from jax.experimental import pallas as pl
from jax.experimental.pallas import tpu as pltpu

F32 = jnp.float32
BF16 = jnp.bfloat16

D_MODEL = 1024
HEADS = 4
HEAD_DIM = 128
DILATIONS = (1, 4, 16)
ATT_BLOCK = 128
GROUP_WIDTH = HEADS * HEAD_DIM
QK_WIDTH = len(DILATIONS) * GROUP_WIDTH
CONF_WIDTH = 512
CONF_TAPS = 31
SHORT_TAPS = 3
FFN_HIDDEN = 2816
PLE_DIM = 256
ROPE_THETA = 10000.0
NEG_INF = -1e30
LN_EPS = 1e-5
DN_ALPHA = float(4 ** 0.25)

ROW_TILE = 512
FFN_CHUNK = 256
CONV_PAD = 32
CONV_ROWS = 64
SUBLANES = 8
LANES = 128
CONV_TAIL = SUBLANES
V7X_VMEM_LIMIT = 56 * 1024 * 1024


def _dot(a, b):
    return jnp.dot(a, b, preferred_element_type=F32)


def _layer_norm(v, g, b):
    mu = jnp.mean(v, axis=-1, keepdims=True)
    c = v - mu
    var = jnp.mean(c * c, axis=-1, keepdims=True)
    return c * lax.rsqrt(var + LN_EPS) * g + b


def _sigmoid(v):
    return 1.0 / (1.0 + jnp.exp(-v))


def _shift_rows_up(v, shift):
    rows = v.shape[0] - SUBLANES
    if shift == 0:
        return v[0:rows]
    tiles = [pltpu.roll(v[k:k + SUBLANES], SUBLANES - shift, 0) for k in range(0, rows + SUBLANES, SUBLANES)]
    low = lax.broadcasted_iota(jnp.int32, (SUBLANES, v.shape[1]), 0) < SUBLANES - shift
    return jnp.concatenate([jnp.where(low, a, b) for a, b in zip(tiles[:-1], tiles[1:])], axis=0)


def _resident(shape):
    return pl.BlockSpec(shape, lambda *_: (0,) * len(shape), pipeline_mode=pl.Buffered(1))


def _layer_slab(stacked, layer):
    return pl.BlockSpec((None,) + stacked.shape[1:], lambda *_: (layer, 0, 0), pipeline_mode=pl.Buffered(1))


def _rows(stacked):
    return stacked.reshape(stacked.shape[0], 1, stacked.shape[1])


def _even_in_kernel(x_ref, w_ref, rope_ref, cw_ref, cb_ref, lg_ref, lb_ref,
                    o0_ref, o1_ref, o2_ref, conv_ref, slab_ref, hist_ref, pre_ref):
    tm = x_ref.shape[1]
    first = CONV_PAD - (CONF_TAPS - 1)

    @pl.when((pl.program_id(0) == 0) & (pl.program_id(1) == 0))
    def _():
        hist_ref[...] = jnp.zeros(hist_ref.shape, F32)

    @pl.when(pl.program_id(1) == 0)
    def _():
        hist_ref[0:CONV_PAD, :] = jnp.zeros((CONV_PAD, CONF_WIDTH), F32)

    xb = x_ref[0].astype(BF16)
    a = _dot(xb, w_ref[:, 3 * QK_WIDTH:3 * QK_WIDTH + CONF_WIDTH])
    gate = _dot(xb, w_ref[:, 3 * QK_WIDTH + CONF_WIDTH:])
    hist_ref[CONV_PAD:CONV_PAD + tm, :] = a * _sigmoid(gate)

    def zero_from(anchor):
        bits = pltpu.bitcast(anchor, jnp.int32)
        return lax.shift_right_logical(lax.shift_right_logical(bits, 16), 16).astype(F32)[0:1, :]

    def conv_block(row_block, lane_block, after):
        lanes = slice(lane_block * LANES, (lane_block + 1) * LANES)
        t0 = row_block * CONV_ROWS
        hold = zero_from(after[0])
        for other in after[1:]:
            hold = hold + zero_from(other)
        win = hist_ref[t0:t0 + CONV_ROWS + CONV_PAD + CONV_TAIL, lanes] + hold
        acc = jnp.broadcast_to(cb_ref[:, lanes], (CONV_ROWS, LANES))
        for s in range(SUBLANES):
            base, shift = divmod(first + s, SUBLANES)
            part = None
            for j in range(s, CONF_TAPS, SUBLANES):
                k0 = (j - s) + base * SUBLANES
                term = cw_ref[j:j + 1, lanes] * win[k0:k0 + CONV_ROWS + SUBLANES]
                part = term if part is None else part + term
            acc = acc + _shift_rows_up(part, shift)
        pre_ref[t0:t0 + CONV_ROWS, lanes] = acc
        return acc[0:SUBLANES]

    conv_blocks = [(rb, lb) for rb in range(tm // CONV_ROWS) for lb in range(CONF_WIDTH // LANES)]
    qkv_dots = [(j, g) for j in range(3) for g in range(len(DILATIONS))]
    per_dot = -(-len(conv_blocks) // len(qkv_dots))
    outs = (o0_ref, o1_ref, o2_ref)
    done = None
    for idx, (j, g) in enumerate(qkv_dots):
        dil = DILATIONS[g]
        c0 = j * QK_WIDTH + g * GROUP_WIDTH
        acc = _dot(xb, w_ref[:, c0:c0 + GROUP_WIDTH])
        slab = slab_ref.at[idx % 2]
        for h in range(HEADS):
            slab[h] = acc[:, h * HEAD_DIM:(h + 1) * HEAD_DIM]
        for h in range(HEADS):
            for p0 in range(0, tm, CONV_ROWS):
                rows = slice(p0, p0 + CONV_ROWS)
                t = slab[h, rows, :]
                if j < 2:
                    t = t * rope_ref[2 * j, rows, :] + pltpu.roll(t, HEAD_DIM // 2, 1) * rope_ref[2 * j + 1, rows, :]
                if dil == 1:
                    o0_ref[0, j, h, rows, :] = t.astype(BF16)
                elif j < 2:
                    slab[h, rows, :] = t
        if dil > 1:
            for h in range(HEADS):
                for r in range(dil):
                    outs[g][0, j, h, r] = slab[h, pl.ds(r, tm // dil, stride=dil), :].astype(BF16)
        after = [acc[tm - SUBLANES:, GROUP_WIDTH - LANES:]] + ([done] if done is not None else [])
        for rb, lb in conv_blocks[idx * per_dot:(idx + 1) * per_dot]:
            done = conv_block(rb, lb, after)
            after = [done]
            if lb == CONF_WIDTH // LANES - 1:
                rows = slice(rb * CONV_ROWS, (rb + 1) * CONV_ROWS)
                y = _layer_norm(pre_ref[rows, :], lg_ref[...], lb_ref[...])
                conv_ref[0, rows, :] = (y * _sigmoid(y)).astype(BF16)
    hist_ref[0:CONV_PAD, :] = hist_ref[tm:tm + CONV_PAD, :]


def _even_in(x, w_in, conv_w, conv_b, ln_g, ln_b, layer, rope_tab):
    b, s, d = x.shape
    tm = ROW_TILE
    qkv_shapes = [jax.ShapeDtypeStruct((b, 3, HEADS, s, HEAD_DIM), BF16)]
    qkv_specs = [pl.BlockSpec((1, 3, HEADS, tm, HEAD_DIM), lambda bi, mi: (bi, 0, 0, mi, 0))]
    for dil in DILATIONS[1:]:
        qkv_shapes.append(jax.ShapeDtypeStruct((b, 3, HEADS, dil, s // dil, HEAD_DIM), BF16))
        qkv_specs.append(pl.BlockSpec((1, 3, HEADS, dil, tm // dil, HEAD_DIM),
                                      lambda bi, mi: (bi, 0, 0, 0, mi, 0)))
    return pl.pallas_call(
        _even_in_kernel,
        grid=(b, s // tm),
        in_specs=[
            pl.BlockSpec((1, tm, d), lambda bi, mi: (bi, mi, 0)),
            _layer_slab(w_in, layer),
            pl.BlockSpec((4, tm, HEAD_DIM), lambda bi, mi: (0, mi, 0)),
        ] + [_layer_slab(a, layer) for a in (conv_w, conv_b, ln_g, ln_b)],
        out_specs=qkv_specs + [pl.BlockSpec((1, tm, CONF_WIDTH), lambda bi, mi: (bi, mi, 0))],
        out_shape=qkv_shapes + [jax.ShapeDtypeStruct((b, s, CONF_WIDTH), BF16)],
        scratch_shapes=[pltpu.VMEM((2, HEADS, tm, HEAD_DIM), F32),
                        pltpu.VMEM((CONV_PAD + tm + CONV_TAIL, CONF_WIDTH), F32),
                        pltpu.VMEM((tm, CONF_WIDTH), F32)],
        compiler_params=pltpu.CompilerParams(
            dimension_semantics=("arbitrary", "arbitrary"), vmem_limit_bytes=V7X_VMEM_LIMIT),
        name="even_in_proj_conformer",
    )(x, w_in, rope_tab, conv_w, conv_b, ln_g, ln_b)


ATT_BATCH = 8


def _attn_kernel(g0_ref, g1_ref, g2_ref, o_ref, kk_ref, vv_ref, out_ref, lse_ref):
    seq = o_ref.shape[1]
    nb = ATT_BLOCK
    n_blocks = seq // nb
    qi = lax.broadcasted_iota(jnp.int32, (nb, 1), 0)
    kj = lax.broadcasted_iota(jnp.int32, (1, 2 * nb), 1)
    cur_only = (kj >= nb) & (kj <= qi + nb)
    band = (kj >= qi) & (kj <= qi + nb)

    for g, ref, dil in zip(range(3), (g0_ref, g1_ref, g2_ref), DILATIONS):
        per_res = n_blocks // dil
        firsts = range(0, n_blocks, per_res)
        if per_res > 1:
            for j, win_ref in ((1, kk_ref), (2, vv_ref)):
                win_ref[:, nb:, :] = ref[0, j, 0]
                win_ref[1:, 0:nb, :] = ref[0, j, 0, 0:n_blocks - 1]
                for i in firsts:
                    win_ref[i, 0:nb, :] = jnp.zeros((nb, HEAD_DIM), BF16)
            keys, vals, lo = (lambda i: kk_ref[i]), (lambda i: vv_ref[i]), 0
        else:
            keys, vals, lo = (lambda i: ref[0, 1, 0, i]), (lambda i: ref[0, 2, 0, i]), nb

        for c0 in range(0, n_blocks, ATT_BATCH):
            blocks = list(range(c0, c0 + ATT_BATCH))
            s = [lax.dot_general(ref[0, 0, 0, i], keys(i), (((1,), (1,)), ((), ())),
                                 preferred_element_type=F32) for i in blocks]
            masks = [(cur_only if i in firsts else band)[:, lo:] for i in blocks]
            s = [jnp.where(mk, v, NEG_INF) for mk, v in zip(masks, s)]
            m = [jnp.max(v, axis=-1, keepdims=True) for v in s]
            ex = [jnp.exp2(v - mx) for v, mx in zip(s, m)]
            dens = [jnp.sum(e, axis=-1, keepdims=True) for e in ex]
            acc = [_dot(e.astype(BF16), vals(i)) for e, i in zip(ex, blocks)]
            for i, a, mx, den in zip(blocks, acc, m, dens):
                start = (i % per_res) * nb * dil + i // per_res
                rows = pl.ds(start, nb) if dil == 1 else pl.ds(start, nb, stride=dil)
                out_ref[g, rows, :] = a[:, 0:nb] / den
                lse_ref[g, rows, :] = jnp.broadcast_to(mx + jnp.log2(den), (nb, HEAD_DIM))

    chunk = 256
    for c in range(seq // chunk):
        rows = pl.ds(c * chunk, chunk)
        lse = [lse_ref[g, rows, :] for g in range(3)]
        top = jnp.maximum(jnp.maximum(lse[0], lse[1]), lse[2])
        num = jnp.zeros((chunk, HEAD_DIM), F32)
        den = jnp.zeros((chunk, HEAD_DIM), F32)
        for g in range(3):
            a = jnp.exp2(lse[g] - top)
            num = num + a * out_ref[g, rows, :]
            den = den + a
        o_ref[0, rows, :] = (num / den).astype(o_ref.dtype)


def _attention(g0, g1, g2):
    b, _, _, s, e = g0.shape
    nb = ATT_BLOCK
    blocked = [a.reshape(b, 3, HEADS, s // nb, nb, e) for a in (g0, g1, g2)]
    spec = pl.BlockSpec((1, 3, 1, s // nb, nb, e), lambda bi, hi: (bi, 0, hi, 0, 0, 0))
    return pl.pallas_call(
        _attn_kernel,
        grid=(b, HEADS),
        in_specs=[spec] * 3,
        out_specs=pl.BlockSpec((1, s, e), lambda bi, hi: (bi, 0, hi)),
        out_shape=jax.ShapeDtypeStruct((b, s, GROUP_WIDTH), BF16),
        scratch_shapes=[pltpu.VMEM((s // nb, 2 * nb, e), BF16)] * 2 + [pltpu.VMEM((3, s, e), F32)] * 2,
        compiler_params=pltpu.CompilerParams(
            dimension_semantics=("arbitrary", "arbitrary"), vmem_limit_bytes=V7X_VMEM_LIMIT),
        name="dilated_attention",
    )(*blocked)


def _swiglu_ple(x1, p_rows, hid_ref, w_in_ref, w_out_ref, g_ref, b_ref, wp_ref, wg_ref):
    xb = x1.astype(BF16)
    for c in range(FFN_HIDDEN // FFN_CHUNK):
        cols = slice(c * FFN_CHUNK, (c + 1) * FFN_CHUNK)
        gate = _dot(xb, w_in_ref[:, cols])
        up = _dot(xb, w_in_ref[:, FFN_HIDDEN + c * FFN_CHUNK:FFN_HIDDEN + (c + 1) * FFN_CHUNK])
        hid_ref[:, cols] = (gate * _sigmoid(gate) * up).astype(BF16)
    y = _dot(hid_ref[...], w_out_ref[...])
    x2 = _layer_norm(DN_ALPHA * x1 + y, g_ref[...], b_ref[...])
    emb = _dot(p_rows.astype(BF16), wp_ref[...])
    return x2 + emb * _sigmoid(_dot(x2.astype(BF16), wg_ref[...]))


def _ffn_slabs(ffn_params, layer):
    return [_layer_slab(a, layer) for a in ffn_params]


def _even_tail_kernel(attn_ref, conv_ref, x_ref, p_ref, w_ref, mg_ref, mb_ref, *rest):
    *ffn_refs, o_ref, hid_ref = rest
    mix = _dot(attn_ref[...], w_ref[0:GROUP_WIDTH, :]) + _dot(conv_ref[...], w_ref[GROUP_WIDTH:, :])
    x1 = _layer_norm(DN_ALPHA * x_ref[...] + mix, mg_ref[...], mb_ref[...])
    o_ref[...] = _swiglu_ple(x1, p_ref[...], hid_ref, *ffn_refs)


def _even_tail(attn, conv, x2d, p_all, w_out, mixer_layer, ln_g, ln_b, ffn_params, layer):
    n, d = x2d.shape
    tm = ROW_TILE
    return pl.pallas_call(
        _even_tail_kernel,
        grid=(n // tm,),
        in_specs=[
            pl.BlockSpec((tm, GROUP_WIDTH), lambda i: (i, 0)),
            pl.BlockSpec((tm, CONF_WIDTH), lambda i: (i, 0)),
            pl.BlockSpec((tm, d), lambda i: (i, 0)),
            pl.BlockSpec((None, tm, PLE_DIM), lambda i: (layer, i, 0)),
            _layer_slab(w_out, mixer_layer), _layer_slab(ln_g, layer), _layer_slab(ln_b, layer),
        ] + _ffn_slabs(ffn_params, layer),
        out_specs=pl.BlockSpec((tm, d), lambda i: (i, 0)),
        out_shape=jax.ShapeDtypeStruct((n, d), F32),
        scratch_shapes=[pltpu.VMEM((tm, FFN_HIDDEN), BF16)],
        compiler_params=pltpu.CompilerParams(
            dimension_semantics=("arbitrary",), vmem_limit_bytes=V7X_VMEM_LIMIT),
        name="even_out_proj_swiglu_ple",
    )(attn, conv, x2d, p_all, w_out, ln_g, ln_b, *ffn_params)


SHORT_CARRY = 8


def _odd_layer_kernel(x_ref, p_ref, w_in_ref, cw_ref, w_out_ref, mg_ref, mb_ref, *rest):
    *ffn_refs, o_ref, gate_ref, mix_ref, hid_ref = rest
    tm = x_ref.shape[1]
    width = D_MODEL
    chunk = 512

    @pl.when(pl.program_id(1) == 0)
    def _():
        gate_ref[0:SHORT_CARRY, :] = jnp.zeros((SHORT_CARRY, width), F32)

    xb = x_ref[0].astype(BF16)
    for c in range(width // chunk):
        cols = slice(c * chunk, (c + 1) * chunk)
        cg = _dot(xb, w_in_ref[:, width + c * chunk:width + (c + 1) * chunk])
        hh = _dot(xb, w_in_ref[:, 2 * width + c * chunk:2 * width + (c + 1) * chunk])
        gate_ref[SHORT_CARRY:, cols] = cg * hh
        y = jnp.zeros((tm, chunk), F32)
        for j in range(SHORT_TAPS):
            off = SHORT_CARRY - (SHORT_TAPS - 1) + j
            y = y + cw_ref[j:j + 1, cols] * gate_ref[off:off + tm, cols]
        bg = _dot(xb, w_in_ref[:, cols])
        mix_ref[:, cols] = (bg * y).astype(BF16)
        gate_ref[0:SHORT_CARRY, cols] = gate_ref[tm:tm + SHORT_CARRY, cols]
    mix = _dot(mix_ref[...], w_out_ref[...])
    x1 = _layer_norm(DN_ALPHA * x_ref[0] + mix, mg_ref[...], mb_ref[...])
    o_ref[0] = _swiglu_ple(x1, p_ref[...], hid_ref, *ffn_refs)


def _odd_layer(x, p_all, w_in, conv_w, w_out, mixer_layer, ln_g, ln_b, ffn_params, layer):
    b, s, d = x.shape
    tm = ROW_TILE
    tiles = s // tm
    return pl.pallas_call(
        _odd_layer_kernel,
        grid=(b, tiles),
        in_specs=[
            pl.BlockSpec((1, tm, d), lambda bi, mi: (bi, mi, 0)),
            pl.BlockSpec((None, tm, PLE_DIM), lambda bi, mi: (layer, bi * tiles + mi, 0)),
            _layer_slab(w_in, mixer_layer), _layer_slab(conv_w, mixer_layer), _layer_slab(w_out, mixer_layer),
            _layer_slab(ln_g, layer), _layer_slab(ln_b, layer),
        ] + _ffn_slabs(ffn_params, layer),
        out_specs=pl.BlockSpec((1, tm, d), lambda bi, mi: (bi, mi, 0)),
        out_shape=jax.ShapeDtypeStruct((b, s, d), F32),
        scratch_shapes=[pltpu.VMEM((SHORT_CARRY + tm, d), F32), pltpu.VMEM((tm, d), BF16),
                        pltpu.VMEM((tm, FFN_HIDDEN), BF16)],
        compiler_params=pltpu.CompilerParams(
            dimension_semantics=("arbitrary", "arbitrary"), vmem_limit_bytes=V7X_VMEM_LIMIT),
        name="short_conv_mixer_swiglu_ple",
    )(x, p_all, w_in, conv_w, w_out, ln_g, ln_b, *ffn_params)


def _rope_tables(seq):
    half = HEAD_DIM // 2
    inv = ROPE_THETA ** (-np.arange(half, dtype=np.float64) / half)
    ang = np.arange(seq, dtype=np.float64)[:, None] * inv[None, :]
    cos = np.concatenate([np.cos(ang), np.cos(ang)], axis=-1)
    sin = np.concatenate([-np.sin(ang), np.sin(ang)], axis=-1)
    scale = HEAD_DIM ** -0.5 * np.log2(np.e)
    return np.stack([cos * scale, sin * scale, cos, sin]).astype(np.float32)


def kernel(x, p, even_w_in, even_w_out, conf_conv_w, conf_conv_b, conf_ln_g, conf_ln_b, odd_w_in, odd_conv_w, odd_w_out, ln_mix_g, ln_mix_b, ln_ffn_g, ln_ffn_b, ffn_w_in, ffn_w_out, ple_w_proj, ple_w_gate):
    b, s, d = x.shape
    depth = p.shape[0]
    rope_tab = _rope_tables(s)
    p_all = p.reshape(depth, b * s, PLE_DIM)
    even_w_in, even_w_out, odd_w_in, odd_w_out, ffn_w_in, ffn_w_out, ple_w_proj, ple_w_gate = (
        w.astype(BF16) for w in
        (even_w_in, even_w_out, odd_w_in, odd_w_out, ffn_w_in, ffn_w_out, ple_w_proj, ple_w_gate))
    conf_conv_b, conf_ln_g, conf_ln_b, ln_mix_g, ln_mix_b, ln_ffn_g, ln_ffn_b = (
        _rows(v) for v in (conf_conv_b, conf_ln_g, conf_ln_b, ln_mix_g, ln_mix_b, ln_ffn_g, ln_ffn_b))
    for i in range(depth):
        j = i // 2
        ffn_params = (ffn_w_in, ffn_w_out, ln_ffn_g, ln_ffn_b, ple_w_proj, ple_w_gate)
        if i % 2 == 0:
            g0, g1, g2, conv = _even_in(x, even_w_in, conf_conv_w, conf_conv_b, conf_ln_g, conf_ln_b, j, rope_tab)
            attn = _attention(g0, g1, g2)
            x = _even_tail(attn.reshape(b * s, -1), conv.reshape(b * s, -1), x.reshape(b * s, d), p_all,
                           even_w_out, j, ln_mix_g, ln_mix_b, ffn_params, i).reshape(b, s, d)
        else:
            x = _odd_layer(x, p_all, odd_w_in, odd_conv_w, odd_w_out, j, ln_mix_g, ln_mix_b, ffn_params, i)
    return x
```

```python
import functools

import jax
import jax.numpy as jnp
import numpy as np
from jax import lax
from jax.experimental import pallas as pl
from jax.experimental.pallas import tpu as pltpu

F32 = jnp.float32
BF16 = jnp.bfloat16

D_MODEL = 1024
HEADS = 4
HEAD_DIM = 128
DILATIONS = (1, 4, 16)
ATT_BLOCK = 128
GROUP_WIDTH = HEADS * HEAD_DIM
QK_WIDTH = len(DILATIONS) * GROUP_WIDTH
CONF_WIDTH = 512
CONF_TAPS = 31
SHORT_TAPS = 3
FFN_HIDDEN = 2816
PLE_DIM = 256
ROPE_THETA = 10000.0
NEG_INF = -1e30
LN_EPS = 1e-5
DN_ALPHA = float(4 ** 0.25)

ROW_TILE = 512
FFN_CHUNK = 256
CONV_PAD = 32
CONV_ROWS = 64
SUBLANES = 8
LANES = 128
CONV_TAIL = SUBLANES
V7X_VMEM_LIMIT = 56 * 1024 * 1024


def _dot(a, b):
    return jnp.dot(a, b, preferred_element_type=F32)


def _layer_norm(v, g, b):
    mu = jnp.mean(v, axis=-1, keepdims=True)
    c = v - mu
    var = jnp.mean(c * c, axis=-1, keepdims=True)
    return c * lax.rsqrt(var + LN_EPS) * g + b


def _sigmoid(v):
    return 1.0 / (1.0 + jnp.exp(-v))


def _shift_rows_up(v, shift):
    rows = v.shape[0] - SUBLANES
    if shift == 0:
        return v[0:rows]
    tiles = [pltpu.roll(v[k:k + SUBLANES], SUBLANES - shift, 0) for k in range(0, rows + SUBLANES, SUBLANES)]
    low = lax.broadcasted_iota(jnp.int32, (SUBLANES, v.shape[1]), 0) < SUBLANES - shift
    return jnp.concatenate([jnp.where(low, a, b) for a, b in zip(tiles[:-1], tiles[1:])], axis=0)


def _resident(shape):
    return pl.BlockSpec(shape, lambda *_: (0,) * len(shape), pipeline_mode=pl.Buffered(1))


def _layer_slab(stacked, layer):
    return pl.BlockSpec((None,) + stacked.shape[1:], lambda *_: (layer, 0, 0), pipeline_mode=pl.Buffered(1))


def _rows(stacked):
    return stacked.reshape(stacked.shape[0], 1, stacked.shape[1])


def _even_in_kernel(x_ref, w_ref, rope_ref, o0_ref, o1_ref, o2_ref, glu_ref, slab_ref):
    tm = x_ref.shape[1]
    xb = x_ref[0].astype(BF16)
    outs = (o0_ref, o1_ref, o2_ref)
    for j in range(3):
        for g, dil in enumerate(DILATIONS):
            c0 = j * QK_WIDTH + g * GROUP_WIDTH
            acc = _dot(xb, w_ref[:, c0:c0 + GROUP_WIDTH])
            for h in range(HEADS):
                t = acc[:, h * HEAD_DIM:(h + 1) * HEAD_DIM]
                if j < 2:
                    t = t * rope_ref[2 * j] + pltpu.roll(t, HEAD_DIM // 2, 1) * rope_ref[2 * j + 1]
                if dil == 1:
                    o0_ref[0, j, h] = t.astype(BF16)
                else:
                    slab_ref[h] = t
            if dil > 1:
                for h in range(HEADS):
                    for r in range(dil):
                        outs[g][0, j, h, r] = slab_ref[h, pl.ds(r, tm // dil, stride=dil), :].astype(BF16)
    a = _dot(xb, w_ref[:, 3 * QK_WIDTH:3 * QK_WIDTH + CONF_WIDTH])
    gate = _dot(xb, w_ref[:, 3 * QK_WIDTH + CONF_WIDTH:])
    glu_ref[0] = a * _sigmoid(gate)


def _even_in(x, w_in, layer, rope_tab):
    b, s, d = x.shape
    tm = ROW_TILE
    qkv_shapes = [jax.ShapeDtypeStruct((b, 3, HEADS, s, HEAD_DIM), BF16)]
    qkv_specs = [pl.BlockSpec((1, 3, HEADS, tm, HEAD_DIM), lambda bi, mi: (bi, 0, 0, mi, 0))]
    for dil in DILATIONS[1:]:
        qkv_shapes.append(jax.ShapeDtypeStruct((b, 3, HEADS, dil, s // dil, HEAD_DIM), BF16))
        qkv_specs.append(pl.BlockSpec((1, 3, HEADS, dil, tm // dil, HEAD_DIM),
                                      lambda bi, mi: (bi, 0, 0, 0, mi, 0)))
    return pl.pallas_call(
        _even_in_kernel,
        grid=(b, s // tm),
        in_specs=[
            pl.BlockSpec((1, tm, d), lambda bi, mi: (bi, mi, 0)),
            _layer_slab(w_in, layer),
            pl.BlockSpec((4, tm, HEAD_DIM), lambda bi, mi: (0, mi, 0)),
        ],
        out_specs=qkv_specs + [pl.BlockSpec((1, tm, CONF_WIDTH), lambda bi, mi: (bi, mi, 0))],
        out_shape=qkv_shapes + [jax.ShapeDtypeStruct((b, s, CONF_WIDTH), F32)],
        scratch_shapes=[pltpu.VMEM((HEADS, tm, HEAD_DIM), F32)],
        compiler_params=pltpu.CompilerParams(
            dimension_semantics=("arbitrary", "arbitrary"), vmem_limit_bytes=V7X_VMEM_LIMIT),
        name="even_in_proj",
    )(x, w_in, rope_tab)


ATT_BATCH = 8


def _attn_kernel(g0_ref, g1_ref, g2_ref, o_ref, kk_ref, vv_ref, out_ref, lse_ref):
    seq = o_ref.shape[1]
    nb = ATT_BLOCK
    n_blocks = seq // nb
    qi = lax.broadcasted_iota(jnp.int32, (nb, 1), 0)
    kj = lax.broadcasted_iota(jnp.int32, (1, 2 * nb), 1)
    cur_only = (kj >= nb) & (kj <= qi + nb)
    band = (kj >= qi) & (kj <= qi + nb)

    for g, ref, dil in zip(range(3), (g0_ref, g1_ref, g2_ref), DILATIONS):
        per_res = n_blocks // dil
        firsts = range(0, n_blocks, per_res)
        if per_res > 1:
            for j, win_ref in ((1, kk_ref), (2, vv_ref)):
                win_ref[:, nb:, :] = ref[0, j, 0]
                win_ref[1:, 0:nb, :] = ref[0, j, 0, 0:n_blocks - 1]
                for i in firsts:
                    win_ref[i, 0:nb, :] = jnp.zeros((nb, HEAD_DIM), BF16)
            keys, vals, lo = (lambda i: kk_ref[i]), (lambda i: vv_ref[i]), 0
        else:
            keys, vals, lo = (lambda i: ref[0, 1, 0, i]), (lambda i: ref[0, 2, 0, i]), nb

        for c0 in range(0, n_blocks, ATT_BATCH):
            blocks = list(range(c0, c0 + ATT_BATCH))
            s = [lax.dot_general(ref[0, 0, 0, i], keys(i), (((1,), (1,)), ((), ())),
                                 preferred_element_type=F32) for i in blocks]
            masks = [(cur_only if i in firsts else band)[:, lo:] for i in blocks]
            s = [jnp.where(mk, v, NEG_INF) for mk, v in zip(masks, s)]
            m = [jnp.max(v, axis=-1, keepdims=True) for v in s]
            ex = [jnp.exp2(v - mx) for v, mx in zip(s, m)]
            dens = [jnp.sum(e, axis=-1, keepdims=True) for e in ex]
            acc = [_dot(e.astype(BF16), vals(i)) for e, i in zip(ex, blocks)]
            for i, a, mx, den in zip(blocks, acc, m, dens):
                start = (i % per_res) * nb * dil + i // per_res
                rows = pl.ds(start, nb) if dil == 1 else pl.ds(start, nb, stride=dil)
                out_ref[g, rows, :] = a[:, 0:nb] / den
                lse_ref[g, rows, :] = jnp.broadcast_to(mx + jnp.log2(den), (nb, HEAD_DIM))

    chunk = 256
    for c in range(seq // chunk):
        rows = pl.ds(c * chunk, chunk)
        lse = [lse_ref[g, rows, :] for g in range(3)]
        top = jnp.maximum(jnp.maximum(lse[0], lse[1]), lse[2])
        num = jnp.zeros((chunk, HEAD_DIM), F32)
        den = jnp.zeros((chunk, HEAD_DIM), F32)
        for g in range(3):
            a = jnp.exp2(lse[g] - top)
            num = num + a * out_ref[g, rows, :]
            den = den + a
        o_ref[0, rows, :] = (num / den).astype(o_ref.dtype)


def _attention(g0, g1, g2):
    b, _, _, s, e = g0.shape
    nb = ATT_BLOCK
    blocked = [a.reshape(b, 3, HEADS, s // nb, nb, e) for a in (g0, g1, g2)]
    spec = pl.BlockSpec((1, 3, 1, s // nb, nb, e), lambda bi, hi: (bi, 0, hi, 0, 0, 0))
    return pl.pallas_call(
        _attn_kernel,
        grid=(b, HEADS),
        in_specs=[spec] * 3,
        out_specs=pl.BlockSpec((1, s, e), lambda bi, hi: (bi, 0, hi)),
        out_shape=jax.ShapeDtypeStruct((b, s, GROUP_WIDTH), BF16),
        scratch_shapes=[pltpu.VMEM((s // nb, 2 * nb, e), BF16)] * 2 + [pltpu.VMEM((3, s, e), F32)] * 2,
        compiler_params=pltpu.CompilerParams(
            dimension_semantics=("arbitrary", "arbitrary"), vmem_limit_bytes=V7X_VMEM_LIMIT),
        name="dilated_attention",
    )(*blocked)


def _swiglu_ple(x1, p_rows, hid_ref, w_in_ref, w_out_ref, g_ref, b_ref, wp_ref, wg_ref, side_work=None):
    xb = x1.astype(BF16)
    for c in range(FFN_HIDDEN // FFN_CHUNK):
        cols = slice(c * FFN_CHUNK, (c + 1) * FFN_CHUNK)
        gate = _dot(xb, w_in_ref[:, cols])
        up = _dot(xb, w_in_ref[:, FFN_HIDDEN + c * FFN_CHUNK:FFN_HIDDEN + (c + 1) * FFN_CHUNK])
        hid_ref[:, cols] = (gate * _sigmoid(gate) * up).astype(BF16)
        if side_work is not None:
            side_work(c, gate[gate.shape[0] - SUBLANES:, 0:LANES])
    y = _dot(hid_ref[...], w_out_ref[...])
    x2 = _layer_norm(DN_ALPHA * x1 + y, g_ref[...], b_ref[...])
    emb = _dot(p_rows.astype(BF16), wp_ref[...])
    return x2 + emb * _sigmoid(_dot(x2.astype(BF16), wg_ref[...]))


def _ffn_slabs(ffn_params, layer):
    return [_layer_slab(a, layer) for a in ffn_params]


def _zero_from(anchor):
    bits = pltpu.bitcast(anchor, jnp.int32)
    return lax.shift_right_logical(lax.shift_right_logical(bits, 16), 16).astype(F32)[0:1, :]


def _conv_block(hist_ref, pre_ref, cw_ref, cb_ref, row_block, lane_block, after):
    first = CONV_PAD - (CONF_TAPS - 1)
    lanes = slice(lane_block * LANES, (lane_block + 1) * LANES)
    t0 = row_block * CONV_ROWS
    win = hist_ref[t0:t0 + CONV_ROWS + CONV_PAD + CONV_TAIL, lanes]
    for value in after:
        win = win + _zero_from(value)
    acc = jnp.broadcast_to(cb_ref[:, lanes], (CONV_ROWS, LANES))
    for s in range(SUBLANES):
        base, shift = divmod(first + s, SUBLANES)
        part = None
        for j in range(s, CONF_TAPS, SUBLANES):
            k0 = (j - s) + base * SUBLANES
            term = cw_ref[j:j + 1, lanes] * win[k0:k0 + CONV_ROWS + SUBLANES]
            part = term if part is None else part + term
        acc = acc + _shift_rows_up(part, shift)
    pre_ref[t0:t0 + CONV_ROWS, lanes] = acc
    return acc[0:SUBLANES]


def _conv_blocks(hist_ref, pre_ref, dst_ref, conv_refs, blocks, after):
    cw_ref, cb_ref, lg_ref, lb_ref = conv_refs
    for rb, lb in blocks:
        after = [_conv_block(hist_ref, pre_ref, cw_ref, cb_ref, rb, lb, after)]
        if lb == CONF_WIDTH // LANES - 1:
            rows = slice(rb * CONV_ROWS, (rb + 1) * CONV_ROWS)
            y = _layer_norm(pre_ref[rows, :], lg_ref[...], lb_ref[...])
            dst_ref[rows, :] = (y * _sigmoid(y)).astype(BF16)
    return after


def _even_tail_kernel(tiles_per_seq, attn_ref, glu0_ref, glu_next_ref, x_ref, p_ref, w_ref, mg_ref, mb_ref,
                      cw_ref, cb_ref, lg_ref, lb_ref, *rest):
    *ffn_refs, o_ref, hid_ref, hist_ref, pre_ref, conv_cur_ref, conv_next_ref = rest
    tm = x_ref.shape[0]
    step = pl.program_id(0)
    conv_refs = (cw_ref, cb_ref, lg_ref, lb_ref)
    blocks = [(rb, lb) for rb in range(tm // CONV_ROWS) for lb in range(CONF_WIDTH // LANES)]

    def stage(glu_ref, conv_tile):
        hist_ref[CONV_PAD:CONV_PAD + tm, :] = glu_ref[...]
        return (conv_tile + 1) % tiles_per_seq != 0

    def carry(keep):
        tail = hist_ref[tm:tm + CONV_PAD, :]
        hist_ref[0:CONV_PAD, :] = jnp.where(keep, tail, 0.0)

    @pl.when(step == 0)
    def _():
        hist_ref[...] = jnp.zeros(hist_ref.shape, F32)
        keep = stage(glu0_ref, step)
        zero = jnp.zeros((SUBLANES, LANES), F32)
        _conv_blocks(hist_ref, pre_ref, conv_cur_ref, conv_refs, blocks, [zero])
        carry(keep)

    mix = _dot(attn_ref[...], w_ref[0:GROUP_WIDTH, :]) + _dot(conv_cur_ref[...], w_ref[GROUP_WIDTH:, :])
    x1 = _layer_norm(DN_ALPHA * x_ref[...] + mix, mg_ref[...], mb_ref[...])

    keep = stage(glu_next_ref, step + 1)
    chunks = FFN_HIDDEN // FFN_CHUNK
    per_chunk = -(-len(blocks) // chunks)
    state = {"after": []}

    def side_work(c, anchor):
        todo = blocks[c * per_chunk:(c + 1) * per_chunk]
        state["after"] = _conv_blocks(hist_ref, pre_ref, conv_next_ref, conv_refs, todo, [anchor] + state["after"])

    o_ref[...] = _swiglu_ple(x1, p_ref[...], hid_ref, *ffn_refs, side_work=side_work)
    carry(keep)
    conv_cur_ref[...] = conv_next_ref[...]


def _even_tail(attn, glu, x2d, p_all, w_out, conv_params, mixer_layer, ln_g, ln_b, ffn_params, layer, tiles_per_seq):
    n, d = x2d.shape
    tm = ROW_TILE
    last = n // tm - 1
    return pl.pallas_call(
        functools.partial(_even_tail_kernel, tiles_per_seq),
        grid=(n // tm,),
        in_specs=[
            pl.BlockSpec((tm, GROUP_WIDTH), lambda i: (i, 0)),
            pl.BlockSpec((tm, CONF_WIDTH), lambda i: (0, 0)),
            pl.BlockSpec((tm, CONF_WIDTH), lambda i: (jnp.minimum(i + 1, last), 0)),
            pl.BlockSpec((tm, d), lambda i: (i, 0)),
            pl.BlockSpec((None, tm, PLE_DIM), lambda i: (layer, i, 0)),
            _layer_slab(w_out, mixer_layer), _layer_slab(ln_g, layer), _layer_slab(ln_b, layer),
        ] + [_layer_slab(a, mixer_layer) for a in conv_params] + _ffn_slabs(ffn_params, layer),
        out_specs=pl.BlockSpec((tm, d), lambda i: (i, 0)),
        out_shape=jax.ShapeDtypeStruct((n, d), F32),
        scratch_shapes=[pltpu.VMEM((tm, FFN_HIDDEN), BF16),
                        pltpu.VMEM((CONV_PAD + tm + CONV_TAIL, CONF_WIDTH), F32),
                        pltpu.VMEM((tm, CONF_WIDTH), F32),
                        pltpu.VMEM((tm, CONF_WIDTH), BF16), pltpu.VMEM((tm, CONF_WIDTH), BF16)],
        compiler_params=pltpu.CompilerParams(
            dimension_semantics=("arbitrary",), vmem_limit_bytes=V7X_VMEM_LIMIT),
        name="conformer_out_proj_swiglu_ple",
    )(attn, glu, glu, x2d, p_all, w_out, ln_g, ln_b, *conv_params, *ffn_params)


SHORT_CARRY = 8


def _odd_layer_kernel(x_ref, p_ref, w_in_ref, cw_ref, w_out_ref, mg_ref, mb_ref, *rest):
    *ffn_refs, o_ref, gate_ref, mix_ref, hid_ref = rest
    tm = x_ref.shape[1]
    width = D_MODEL
    chunk = 512

    @pl.when(pl.program_id(1) == 0)
    def _():
        gate_ref[0:SHORT_CARRY, :] = jnp.zeros((SHORT_CARRY, width), F32)

    xb = x_ref[0].astype(BF16)
    for c in range(width // chunk):
        cols = slice(c * chunk, (c + 1) * chunk)
        cg = _dot(xb, w_in_ref[:, width + c * chunk:width + (c + 1) * chunk])
        hh = _dot(xb, w_in_ref[:, 2 * width + c * chunk:2 * width + (c + 1) * chunk])
        gate_ref[SHORT_CARRY:, cols] = cg * hh
        y = jnp.zeros((tm, chunk), F32)
        for j in range(SHORT_TAPS):
            off = SHORT_CARRY - (SHORT_TAPS - 1) + j
            y = y + cw_ref[j:j + 1, cols] * gate_ref[off:off + tm, cols]
        bg = _dot(xb, w_in_ref[:, cols])
        mix_ref[:, cols] = (bg * y).astype(BF16)
        gate_ref[0:SHORT_CARRY, cols] = gate_ref[tm:tm + SHORT_CARRY, cols]
    mix = _dot(mix_ref[...], w_out_ref[...])
    x1 = _layer_norm(DN_ALPHA * x_ref[0] + mix, mg_ref[...], mb_ref[...])
    o_ref[0] = _swiglu_ple(x1, p_ref[...], hid_ref, *ffn_refs)


def _odd_layer(x, p_all, w_in, conv_w, w_out, mixer_layer, ln_g, ln_b, ffn_params, layer):
    b, s, d = x.shape
    tm = ROW_TILE
    tiles = s // tm
    return pl.pallas_call(
        _odd_layer_kernel,
        grid=(b, tiles),
        in_specs=[
            pl.BlockSpec((1, tm, d), lambda bi, mi: (bi, mi, 0)),
            pl.BlockSpec((None, tm, PLE_DIM), lambda bi, mi: (layer, bi * tiles + mi, 0)),
            _layer_slab(w_in, mixer_layer), _layer_slab(conv_w, mixer_layer), _layer_slab(w_out, mixer_layer),
            _layer_slab(ln_g, layer), _layer_slab(ln_b, layer),
        ] + _ffn_slabs(ffn_params, layer),
        out_specs=pl.BlockSpec((1, tm, d), lambda bi, mi: (bi, mi, 0)),
        out_shape=jax.ShapeDtypeStruct((b, s, d), F32),
        scratch_shapes=[pltpu.VMEM((SHORT_CARRY + tm, d), F32), pltpu.VMEM((tm, d), BF16),
                        pltpu.VMEM((tm, FFN_HIDDEN), BF16)],
        compiler_params=pltpu.CompilerParams(
            dimension_semantics=("arbitrary", "arbitrary"), vmem_limit_bytes=V7X_VMEM_LIMIT),
        name="short_conv_mixer_swiglu_ple",
    )(x, p_all, w_in, conv_w, w_out, ln_g, ln_b, *ffn_params)


def _rope_tables(seq):
    half = HEAD_DIM // 2
    inv = ROPE_THETA ** (-np.arange(half, dtype=np.float64) / half)
    ang = np.arange(seq, dtype=np.float64)[:, None] * inv[None, :]
    cos = np.concatenate([np.cos(ang), np.cos(ang)], axis=-1)
    sin = np.concatenate([-np.sin(ang), np.sin(ang)], axis=-1)
    scale = HEAD_DIM ** -0.5 * np.log2(np.e)
    return np.stack([cos * scale, sin * scale, cos, sin]).astype(np.float32)


def kernel(x, p, even_w_in, even_w_out, conf_conv_w, conf_conv_b, conf_ln_g, conf_ln_b, odd_w_in, odd_conv_w, odd_w_out, ln_mix_g, ln_mix_b, ln_ffn_g, ln_ffn_b, ffn_w_in, ffn_w_out, ple_w_proj, ple_w_gate):
    b, s, d = x.shape
    depth = p.shape[0]
    rope_tab = _rope_tables(s)
    p_all = p.reshape(depth, b * s, PLE_DIM)
    even_w_in, even_w_out, odd_w_in, odd_w_out, ffn_w_in, ffn_w_out, ple_w_proj, ple_w_gate = (
        w.astype(BF16) for w in
        (even_w_in, even_w_out, odd_w_in, odd_w_out, ffn_w_in, ffn_w_out, ple_w_proj, ple_w_gate))
    conf_conv_b, conf_ln_g, conf_ln_b, ln_mix_g, ln_mix_b, ln_ffn_g, ln_ffn_b = (
        _rows(v) for v in (conf_conv_b, conf_ln_g, conf_ln_b, ln_mix_g, ln_mix_b, ln_ffn_g, ln_ffn_b))
    for i in range(depth):
        j = i // 2
        ffn_params = (ffn_w_in, ffn_w_out, ln_ffn_g, ln_ffn_b, ple_w_proj, ple_w_gate)
        if i % 2 == 0:
            g0, g1, g2, glu = _even_in(x, even_w_in, j, rope_tab)
            attn = _attention(g0, g1, g2)
            conv_params = (conf_conv_w, conf_conv_b, conf_ln_g, conf_ln_b)
            x = _even_tail(attn.reshape(b * s, -1), glu.reshape(b * s, -1), x.reshape(b * s, d), p_all,
                           even_w_out, conv_params, j, ln_mix_g, ln_mix_b, ffn_params, i,
                           s // ROW_TILE).reshape(b, s, d)
        else:
            x = _odd_layer(x, p_all, odd_w_in, odd_conv_w, odd_w_out, j, ln_mix_g, ln_mix_b, ffn_params, i)
    return x
```

```python
import functools

import jax
import jax.numpy as jnp
import numpy as np
from jax import lax
from jax.experimental import pallas as pl
from jax.experimental.pallas import tpu as pltpu

F32 = jnp.float32
BF16 = jnp.bfloat16

D_MODEL = 1024
HEADS = 4
HEAD_DIM = 128
DILATIONS = (1, 4, 16)
ATT_BLOCK = 128
GROUP_WIDTH = HEADS * HEAD_DIM
QK_WIDTH = len(DILATIONS) * GROUP_WIDTH
CONF_WIDTH = 512
CONF_TAPS = 31
SHORT_TAPS = 3
FFN_HIDDEN = 2816
PLE_DIM = 256
ROPE_THETA = 10000.0
NEG_INF = -1e30
LN_EPS = 1e-5
DN_ALPHA = float(4 ** 0.25)

SPLIT_STRIDE = 4
ROW_TILE = 512
FFN_CHUNK = 256
CONV_PAD = 32
CONV_ROWS = 64
SUBLANES = 8
LANES = 128
CONV_TAIL = SUBLANES
V7X_VMEM_LIMIT = 56 * 1024 * 1024


def _dot(a, b):
    return jnp.dot(a, b, preferred_element_type=F32)


def _layer_norm(v, g, b):
    mu = jnp.mean(v, axis=-1, keepdims=True)
    c = v - mu
    var = jnp.mean(c * c, axis=-1, keepdims=True)
    return c * lax.rsqrt(var + LN_EPS) * g + b


def _sigmoid(v):
    return 1.0 / (1.0 + jnp.exp(-v))


def _shift_rows_up(v, shift):
    rows = v.shape[0] - SUBLANES
    if shift == 0:
        return v[0:rows]
    tiles = [pltpu.roll(v[k:k + SUBLANES], SUBLANES - shift, 0) for k in range(0, rows + SUBLANES, SUBLANES)]
    low = lax.broadcasted_iota(jnp.int32, (SUBLANES, v.shape[1]), 0) < SUBLANES - shift
    return jnp.concatenate([jnp.where(low, a, b) for a, b in zip(tiles[:-1], tiles[1:])], axis=0)


def _resident(shape):
    return pl.BlockSpec(shape, lambda *_: (0,) * len(shape), pipeline_mode=pl.Buffered(1))


def _layer_slab(stacked, layer):
    return pl.BlockSpec((None,) + stacked.shape[1:], lambda *_: (layer, 0, 0), pipeline_mode=pl.Buffered(1))


def _rows(stacked):
    return stacked.reshape(stacked.shape[0], 1, stacked.shape[1])


def _even_in_kernel(x_ref, w_ref, rope_ref, o0_ref, o1_ref, o2_ref, glu_ref, slab_ref):
    tm = x_ref.shape[1]
    xb = x_ref[0].astype(BF16)
    outs = (o0_ref, o1_ref, o2_ref)
    for j in range(3):
        for g, dil in enumerate(DILATIONS):
            c0 = j * QK_WIDTH + g * GROUP_WIDTH
            acc = _dot(xb, w_ref[:, c0:c0 + GROUP_WIDTH])
            slab, regroup = slab_ref.at[0], slab_ref.at[1]
            for h in range(HEADS):
                t = acc[:, h * HEAD_DIM:(h + 1) * HEAD_DIM]
                if j < 2:
                    t = t * rope_ref[2 * j] + pltpu.roll(t, HEAD_DIM // 2, 1) * rope_ref[2 * j + 1]
                if dil == 1:
                    o0_ref[0, j, h] = t.astype(BF16)
                else:
                    slab[h] = t
            if dil == SPLIT_STRIDE:
                for h in range(HEADS):
                    for r in range(dil):
                        outs[g][0, j, h, r] = slab[h, pl.ds(r, tm // dil, stride=dil), :].astype(BF16)
            elif dil == SPLIT_STRIDE * SPLIT_STRIDE:
                part = tm // SPLIT_STRIDE
                for h in range(HEADS):
                    for r in range(SPLIT_STRIDE):
                        regroup[h, r * part:(r + 1) * part, :] = slab[h, pl.ds(r, part, stride=SPLIT_STRIDE), :]
                for h in range(HEADS):
                    for r in range(dil):
                        lo, hi = r % SPLIT_STRIDE, r // SPLIT_STRIDE
                        rows = pl.ds(lo * part + hi, tm // dil, stride=SPLIT_STRIDE)
                        outs[g][0, j, h, r] = regroup[h, rows, :].astype(BF16)
    a = _dot(xb, w_ref[:, 3 * QK_WIDTH:3 * QK_WIDTH + CONF_WIDTH])
    gate = _dot(xb, w_ref[:, 3 * QK_WIDTH + CONF_WIDTH:])
    glu_ref[0] = a * _sigmoid(gate)


def _even_in(x, w_in, layer, rope_tab):
    b, s, d = x.shape
    tm = ROW_TILE
    qkv_shapes = [jax.ShapeDtypeStruct((b, 3, HEADS, s, HEAD_DIM), BF16)]
    qkv_specs = [pl.BlockSpec((1, 3, HEADS, tm, HEAD_DIM), lambda bi, mi: (bi, 0, 0, mi, 0))]
    for dil in DILATIONS[1:]:
        qkv_shapes.append(jax.ShapeDtypeStruct((b, 3, HEADS, dil, s // dil, HEAD_DIM), BF16))
        qkv_specs.append(pl.BlockSpec((1, 3, HEADS, dil, tm // dil, HEAD_DIM),
                                      lambda bi, mi: (bi, 0, 0, 0, mi, 0)))
    return pl.pallas_call(
        _even_in_kernel,
        grid=(b, s // tm),
        in_specs=[
            pl.BlockSpec((1, tm, d), lambda bi, mi: (bi, mi, 0)),
            _layer_slab(w_in, layer),
            pl.BlockSpec((4, tm, HEAD_DIM), lambda bi, mi: (0, mi, 0)),
        ],
        out_specs=qkv_specs + [pl.BlockSpec((1, tm, CONF_WIDTH), lambda bi, mi: (bi, mi, 0))],
        out_shape=qkv_shapes + [jax.ShapeDtypeStruct((b, s, CONF_WIDTH), F32)],
        scratch_shapes=[pltpu.VMEM((2, HEADS, tm, HEAD_DIM), F32)],
        compiler_params=pltpu.CompilerParams(
            dimension_semantics=("arbitrary", "arbitrary"), vmem_limit_bytes=V7X_VMEM_LIMIT),
        name="even_in_proj",
    )(x, w_in, rope_tab)


ATT_BATCH = 8


def _attn_kernel(g0_ref, g1_ref, g2_ref, o_ref, kk_ref, vv_ref, out_ref, lse_ref):
    seq = o_ref.shape[1]
    nb = ATT_BLOCK
    n_blocks = seq // nb
    qi = lax.broadcasted_iota(jnp.int32, (nb, 1), 0)
    kj = lax.broadcasted_iota(jnp.int32, (1, 2 * nb), 1)
    cur_only = (kj >= nb) & (kj <= qi + nb)
    band = (kj >= qi) & (kj <= qi + nb)

    for g, ref, dil in zip(range(3), (g0_ref, g1_ref, g2_ref), DILATIONS):
        per_res = n_blocks // dil
        firsts = range(0, n_blocks, per_res)
        if per_res > 1:
            for j, win_ref in ((1, kk_ref), (2, vv_ref)):
                win_ref[:, nb:, :] = ref[0, j, 0]
                win_ref[1:, 0:nb, :] = ref[0, j, 0, 0:n_blocks - 1]
                for i in firsts:
                    win_ref[i, 0:nb, :] = jnp.zeros((nb, HEAD_DIM), BF16)
            keys, vals, lo = (lambda i: kk_ref[i]), (lambda i: vv_ref[i]), 0
        else:
            keys, vals, lo = (lambda i: ref[0, 1, 0, i]), (lambda i: ref[0, 2, 0, i]), nb

        for c0 in range(0, n_blocks, ATT_BATCH):
            blocks = list(range(c0, c0 + ATT_BATCH))
            s = [lax.dot_general(ref[0, 0, 0, i], keys(i), (((1,), (1,)), ((), ())),
                                 preferred_element_type=F32) for i in blocks]
            masks = [(cur_only if i in firsts else band)[:, lo:] for i in blocks]
            s = [jnp.where(mk, v, NEG_INF) for mk, v in zip(masks, s)]
            m = [jnp.max(v, axis=-1, keepdims=True) for v in s]
            ex = [jnp.exp2(v - mx) for v, mx in zip(s, m)]
            dens = [jnp.sum(e, axis=-1, keepdims=True) for e in ex]
            acc = [_dot(e.astype(BF16), vals(i)) for e, i in zip(ex, blocks)]
            for i, a, mx, den in zip(blocks, acc, m, dens):
                start = (i % per_res) * nb * dil + i // per_res
                rows = pl.ds(start, nb) if dil == 1 else pl.ds(start, nb, stride=dil)
                out_ref[g, rows, :] = a[:, 0:nb] / den
                lse_ref[g, rows, :] = jnp.broadcast_to(mx + jnp.log2(den), (nb, HEAD_DIM))

    chunk = 256
    for c in range(seq // chunk):
        rows = pl.ds(c * chunk, chunk)
        lse = [lse_ref[g, rows, :] for g in range(3)]
        top = jnp.maximum(jnp.maximum(lse[0], lse[1]), lse[2])
        num = jnp.zeros((chunk, HEAD_DIM), F32)
        den = jnp.zeros((chunk, HEAD_DIM), F32)
        for g in range(3):
            a = jnp.exp2(lse[g] - top)
            num = num + a * out_ref[g, rows, :]
            den = den + a
        o_ref[0, rows, :] = (num / den).astype(o_ref.dtype)


def _attention(g0, g1, g2):
    b, _, _, s, e = g0.shape
    nb = ATT_BLOCK
    blocked = [a.reshape(b, 3, HEADS, s // nb, nb, e) for a in (g0, g1, g2)]
    spec = pl.BlockSpec((1, 3, 1, s // nb, nb, e), lambda bi, hi: (bi, 0, hi, 0, 0, 0))
    return pl.pallas_call(
        _attn_kernel,
        grid=(b, HEADS),
        in_specs=[spec] * 3,
        out_specs=pl.BlockSpec((1, s, e), lambda bi, hi: (bi, 0, hi)),
        out_shape=jax.ShapeDtypeStruct((b, s, GROUP_WIDTH), BF16),
        scratch_shapes=[pltpu.VMEM((s // nb, 2 * nb, e), BF16)] * 2 + [pltpu.VMEM((3, s, e), F32)] * 2,
        compiler_params=pltpu.CompilerParams(
            dimension_semantics=("arbitrary", "arbitrary"), vmem_limit_bytes=V7X_VMEM_LIMIT),
        name="dilated_attention",
    )(*blocked)


def _swiglu_ple(x1, p_rows, hid_ref, w_in_ref, w_out_ref, g_ref, b_ref, wp_ref, wg_ref, side_work=None):
    xb = x1.astype(BF16)
    for c in range(FFN_HIDDEN // FFN_CHUNK):
        cols = slice(c * FFN_CHUNK, (c + 1) * FFN_CHUNK)
        gate = _dot(xb, w_in_ref[:, cols])
        up = _dot(xb, w_in_ref[:, FFN_HIDDEN + c * FFN_CHUNK:FFN_HIDDEN + (c + 1) * FFN_CHUNK])
        hid_ref[:, cols] = (gate * _sigmoid(gate) * up).astype(BF16)
        if side_work is not None:
            side_work(c, gate[gate.shape[0] - SUBLANES:, 0:LANES])
    y = _dot(hid_ref[...], w_out_ref[...])
    x2 = _layer_norm(DN_ALPHA * x1 + y, g_ref[...], b_ref[...])
    emb = _dot(p_rows.astype(BF16), wp_ref[...])
    return x2 + emb * _sigmoid(_dot(x2.astype(BF16), wg_ref[...]))


def _ffn_slabs(ffn_params, layer):
    return [_layer_slab(a, layer) for a in ffn_params]


def _zero_from(anchor):
    bits = pltpu.bitcast(anchor, jnp.int32)
    return lax.shift_right_logical(lax.shift_right_logical(bits, 16), 16).astype(F32)[0:1, :]


def _conv_block(hist_ref, pre_ref, cw_ref, cb_ref, row_block, lane_block, after):
    first = CONV_PAD - (CONF_TAPS - 1)
    lanes = slice(lane_block * LANES, (lane_block + 1) * LANES)
    t0 = row_block * CONV_ROWS
    win = hist_ref[t0:t0 + CONV_ROWS + CONV_PAD + CONV_TAIL, lanes]
    for value in after:
        win = win + _zero_from(value)
    acc = jnp.broadcast_to(cb_ref[:, lanes], (CONV_ROWS, LANES))
    for s in range(SUBLANES):
        base, shift = divmod(first + s, SUBLANES)
        part = None
        for j in range(s, CONF_TAPS, SUBLANES):
            k0 = (j - s) + base * SUBLANES
            term = cw_ref[j:j + 1, lanes] * win[k0:k0 + CONV_ROWS + SUBLANES]
            part = term if part is None else part + term
        acc = acc + _shift_rows_up(part, shift)
    pre_ref[t0:t0 + CONV_ROWS, lanes] = acc
    return acc[0:SUBLANES]


def _conv_blocks(hist_ref, pre_ref, dst_ref, conv_refs, blocks, after):
    cw_ref, cb_ref, lg_ref, lb_ref = conv_refs
    for rb, lb in blocks:
        after = [_conv_block(hist_ref, pre_ref, cw_ref, cb_ref, rb, lb, after)]
        if lb == CONF_WIDTH // LANES - 1:
            rows = slice(rb * CONV_ROWS, (rb + 1) * CONV_ROWS)
            y = _layer_norm(pre_ref[rows, :], lg_ref[...], lb_ref[...])
            dst_ref[rows, :] = (y * _sigmoid(y)).astype(BF16)
    return after


def _even_tail_kernel(tiles_per_seq, attn_ref, glu0_ref, glu_next_ref, x_ref, p_ref, w_ref, mg_ref, mb_ref,
                      cw_ref, cb_ref, lg_ref, lb_ref, *rest):
    *ffn_refs, o_ref, hid_ref, hist_ref, pre_ref, conv_cur_ref, conv_next_ref = rest
    tm = x_ref.shape[0]
    step = pl.program_id(0)
    conv_refs = (cw_ref, cb_ref, lg_ref, lb_ref)
    blocks = [(rb, lb) for rb in range(tm // CONV_ROWS) for lb in range(CONF_WIDTH // LANES)]

    def stage(glu_ref, conv_tile):
        hist_ref[CONV_PAD:CONV_PAD + tm, :] = glu_ref[...]
        return (conv_tile + 1) % tiles_per_seq != 0

    def carry(keep):
        tail = hist_ref[tm:tm + CONV_PAD, :]
        hist_ref[0:CONV_PAD, :] = jnp.where(keep, tail, 0.0)

    @pl.when(step == 0)
    def _():
        hist_ref[...] = jnp.zeros(hist_ref.shape, F32)
        keep = stage(glu0_ref, step)
        zero = jnp.zeros((SUBLANES, LANES), F32)
        _conv_blocks(hist_ref, pre_ref, conv_cur_ref, conv_refs, blocks, [zero])
        carry(keep)

    mix = _dot(attn_ref[...], w_ref[0:GROUP_WIDTH, :]) + _dot(conv_cur_ref[...], w_ref[GROUP_WIDTH:, :])
    x1 = _layer_norm(DN_ALPHA * x_ref[...] + mix, mg_ref[...], mb_ref[...])

    keep = stage(glu_next_ref, step + 1)
    chunks = FFN_HIDDEN // FFN_CHUNK
    per_chunk = -(-len(blocks) // chunks)
    state = {"after": []}

    def side_work(c, anchor):
        todo = blocks[c * per_chunk:(c + 1) * per_chunk]
        state["after"] = _conv_blocks(hist_ref, pre_ref, conv_next_ref, conv_refs, todo, [anchor] + state["after"])

    o_ref[...] = _swiglu_ple(x1, p_ref[...], hid_ref, *ffn_refs, side_work=side_work)
    carry(keep)
    conv_cur_ref[...] = conv_next_ref[...]


def _even_tail(attn, glu, x2d, p_all, w_out, conv_params, mixer_layer, ln_g, ln_b, ffn_params, layer, tiles_per_seq):
    n, d = x2d.shape
    tm = ROW_TILE
    last = n // tm - 1
    return pl.pallas_call(
        functools.partial(_even_tail_kernel, tiles_per_seq),
        grid=(n // tm,),
        in_specs=[
            pl.BlockSpec((tm, GROUP_WIDTH), lambda i: (i, 0)),
            pl.BlockSpec((tm, CONF_WIDTH), lambda i: (0, 0)),
            pl.BlockSpec((tm, CONF_WIDTH), lambda i: (jnp.minimum(i + 1, last), 0)),
            pl.BlockSpec((tm, d), lambda i: (i, 0)),
            pl.BlockSpec((None, tm, PLE_DIM), lambda i: (layer, i, 0)),
            _layer_slab(w_out, mixer_layer), _layer_slab(ln_g, layer), _layer_slab(ln_b, layer),
        ] + [_layer_slab(a, mixer_layer) for a in conv_params] + _ffn_slabs(ffn_params, layer),
        out_specs=pl.BlockSpec((tm, d), lambda i: (i, 0)),
        out_shape=jax.ShapeDtypeStruct((n, d), F32),
        scratch_shapes=[pltpu.VMEM((tm, FFN_HIDDEN), BF16),
                        pltpu.VMEM((CONV_PAD + tm + CONV_TAIL, CONF_WIDTH), F32),
                        pltpu.VMEM((tm, CONF_WIDTH), F32),
                        pltpu.VMEM((tm, CONF_WIDTH), BF16), pltpu.VMEM((tm, CONF_WIDTH), BF16)],
        compiler_params=pltpu.CompilerParams(
            dimension_semantics=("arbitrary",), vmem_limit_bytes=V7X_VMEM_LIMIT),
        name="conformer_out_proj_swiglu_ple",
    )(attn, glu, glu, x2d, p_all, w_out, ln_g, ln_b, *conv_params, *ffn_params)


SHORT_CARRY = 8


def _odd_layer_kernel(x_ref, p_ref, w_in_ref, cw_ref, w_out_ref, mg_ref, mb_ref, *rest):
    *ffn_refs, o_ref, gate_ref, mix_ref, hid_ref = rest
    tm = x_ref.shape[1]
    width = D_MODEL
    chunk = 512

    @pl.when(pl.program_id(1) == 0)
    def _():
        gate_ref[0:SHORT_CARRY, :] = jnp.zeros((SHORT_CARRY, width), F32)

    xb = x_ref[0].astype(BF16)
    for c in range(width // chunk):
        cols = slice(c * chunk, (c + 1) * chunk)
        cg = _dot(xb, w_in_ref[:, width + c * chunk:width + (c + 1) * chunk])
        hh = _dot(xb, w_in_ref[:, 2 * width + c * chunk:2 * width + (c + 1) * chunk])
        gate_ref[SHORT_CARRY:, cols] = cg * hh
        y = jnp.zeros((tm, chunk), F32)
        for j in range(SHORT_TAPS):
            off = SHORT_CARRY - (SHORT_TAPS - 1) + j
            y = y + cw_ref[j:j + 1, cols] * gate_ref[off:off + tm, cols]
        bg = _dot(xb, w_in_ref[:, cols])
        mix_ref[:, cols] = (bg * y).astype(BF16)
        gate_ref[0:SHORT_CARRY, cols] = gate_ref[tm:tm + SHORT_CARRY, cols]
    mix = _dot(mix_ref[...], w_out_ref[...])
    x1 = _layer_norm(DN_ALPHA * x_ref[0] + mix, mg_ref[...], mb_ref[...])
    o_ref[0] = _swiglu_ple(x1, p_ref[...], hid_ref, *ffn_refs)


def _odd_layer(x, p_all, w_in, conv_w, w_out, mixer_layer, ln_g, ln_b, ffn_params, layer):
    b, s, d = x.shape
    tm = ROW_TILE
    tiles = s // tm
    return pl.pallas_call(
        _odd_layer_kernel,
        grid=(b, tiles),
        in_specs=[
            pl.BlockSpec((1, tm, d), lambda bi, mi: (bi, mi, 0)),
            pl.BlockSpec((None, tm, PLE_DIM), lambda bi, mi: (layer, bi * tiles + mi, 0)),
            _layer_slab(w_in, mixer_layer), _layer_slab(conv_w, mixer_layer), _layer_slab(w_out, mixer_layer),
            _layer_slab(ln_g, layer), _layer_slab(ln_b, layer),
        ] + _ffn_slabs(ffn_params, layer),
        out_specs=pl.BlockSpec((1, tm, d), lambda bi, mi: (bi, mi, 0)),
        out_shape=jax.ShapeDtypeStruct((b, s, d), F32),
        scratch_shapes=[pltpu.VMEM((SHORT_CARRY + tm, d), F32), pltpu.VMEM((tm, d), BF16),
                        pltpu.VMEM((tm, FFN_HIDDEN), BF16)],
        compiler_params=pltpu.CompilerParams(
            dimension_semantics=("arbitrary", "arbitrary"), vmem_limit_bytes=V7X_VMEM_LIMIT),
        name="short_conv_mixer_swiglu_ple",
    )(x, p_all, w_in, conv_w, w_out, ln_g, ln_b, *ffn_params)


def _rope_tables(seq):
    half = HEAD_DIM // 2
    inv = ROPE_THETA ** (-np.arange(half, dtype=np.float64) / half)
    ang = np.arange(seq, dtype=np.float64)[:, None] * inv[None, :]
    cos = np.concatenate([np.cos(ang), np.cos(ang)], axis=-1)
    sin = np.concatenate([-np.sin(ang), np.sin(ang)], axis=-1)
    scale = HEAD_DIM ** -0.5 * np.log2(np.e)
    return np.stack([cos * scale, sin * scale, cos, sin]).astype(np.float32)


def kernel(x, p, even_w_in, even_w_out, conf_conv_w, conf_conv_b, conf_ln_g, conf_ln_b, odd_w_in, odd_conv_w, odd_w_out, ln_mix_g, ln_mix_b, ln_ffn_g, ln_ffn_b, ffn_w_in, ffn_w_out, ple_w_proj, ple_w_gate):
    b, s, d = x.shape
    depth = p.shape[0]
    rope_tab = _rope_tables(s)
    p_all = p.reshape(depth, b * s, PLE_DIM)
    even_w_in, even_w_out, odd_w_in, odd_w_out, ffn_w_in, ffn_w_out, ple_w_proj, ple_w_gate = (
        w.astype(BF16) for w in
        (even_w_in, even_w_out, odd_w_in, odd_w_out, ffn_w_in, ffn_w_out, ple_w_proj, ple_w_gate))
    conf_conv_b, conf_ln_g, conf_ln_b, ln_mix_g, ln_mix_b, ln_ffn_g, ln_ffn_b = (
        _rows(v) for v in (conf_conv_b, conf_ln_g, conf_ln_b, ln_mix_g, ln_mix_b, ln_ffn_g, ln_ffn_b))
    for i in range(depth):
        j = i // 2
        ffn_params = (ffn_w_in, ffn_w_out, ln_ffn_g, ln_ffn_b, ple_w_proj, ple_w_gate)
        if i % 2 == 0:
            g0, g1, g2, glu = _even_in(x, even_w_in, j, rope_tab)
            attn = _attention(g0, g1, g2)
            conv_params = (conf_conv_w, conf_conv_b, conf_ln_g, conf_ln_b)
            x = _even_tail(attn.reshape(b * s, -1), glu.reshape(b * s, -1), x.reshape(b * s, d), p_all,
                           even_w_out, conv_params, j, ln_mix_g, ln_mix_b, ffn_params, i,
                           s // ROW_TILE).reshape(b, s, d)
        else:
            x = _odd_layer(x, p_all, odd_w_in, odd_conv_w, odd_w_out, j, ln_mix_g, ln_mix_b, ffn_params, i)
    return x
```

```python
import functools

import jax
import jax.numpy as jnp
import numpy as np
from jax import lax
from jax.experimental import pallas as pl
from jax.experimental.pallas import tpu as pltpu

F32 = jnp.float32
BF16 = jnp.bfloat16

D_MODEL = 1024
HEADS = 4
HEAD_DIM = 128
DILATIONS = (1, 4, 16)
ATT_BLOCK = 128
GROUP_WIDTH = HEADS * HEAD_DIM
QK_WIDTH = len(DILATIONS) * GROUP_WIDTH
CONF_WIDTH = 512
CONF_TAPS = 31
SHORT_TAPS = 3
FFN_HIDDEN = 2816
PLE_DIM = 256
ROPE_THETA = 10000.0
NEG_INF = -1e30
LN_EPS = 1e-5
DN_ALPHA = float(4 ** 0.25)

SPLIT_STRIDE = 4
ROW_TILE = 512
FFN_CHUNK = 256
CONV_PAD = 32
CONV_ROWS = 64
SUBLANES = 8
LANES = 128
CONV_TAIL = SUBLANES
V7X_VMEM_LIMIT = 56 * 1024 * 1024


def _dot(a, b):
    return jnp.dot(a, b, preferred_element_type=F32)


def _layer_norm(v, g, b):
    mu = jnp.mean(v, axis=-1, keepdims=True)
    c = v - mu
    var = jnp.mean(c * c, axis=-1, keepdims=True)
    return c * lax.rsqrt(var + LN_EPS) * g + b


def _sigmoid(v):
    return 1.0 / (1.0 + jnp.exp(-v))


def _shift_rows_up(v, shift):
    rows = v.shape[0] - SUBLANES
    if shift == 0:
        return v[0:rows]
    tiles = [pltpu.roll(v[k:k + SUBLANES], SUBLANES - shift, 0) for k in range(0, rows + SUBLANES, SUBLANES)]
    low = lax.broadcasted_iota(jnp.int32, (SUBLANES, v.shape[1]), 0) < SUBLANES - shift
    return jnp.concatenate([jnp.where(low, a, b) for a, b in zip(tiles[:-1], tiles[1:])], axis=0)


def _resident(shape):
    return pl.BlockSpec(shape, lambda *_: (0,) * len(shape), pipeline_mode=pl.Buffered(1))


def _layer_slab(stacked, layer):
    return pl.BlockSpec((None,) + stacked.shape[1:], lambda *_: (layer, 0, 0), pipeline_mode=pl.Buffered(1))


def _rows(stacked):
    return stacked.reshape(stacked.shape[0], 1, stacked.shape[1])


STAGE_ROWS = 1024
STAGE_COLS = 512


def _in_hbm():
    return pl.BlockSpec(memory_space=pl.ANY)


def _weight_scratch(stacked):
    return pltpu.VMEM(stacked.shape[1:], BF16)


def _staging_scratch():
    return [pltpu.VMEM((2, STAGE_ROWS, STAGE_COLS), F32), pltpu.SemaphoreType.DMA((2,))]


def _fetch_weights_bf16(jobs, stage_ref, sem_ref):
    chunks = [(src, dst, r0, min(STAGE_ROWS, dst.shape[0] - r0), c0)
              for src, dst in jobs
              for c0 in range(0, dst.shape[1], STAGE_COLS)
              for r0 in range(0, dst.shape[0], STAGE_ROWS)]

    def copy(i):
        src, _, r0, rows, c0 = chunks[i]
        return pltpu.make_async_copy(src.at[pl.ds(r0, rows), pl.ds(c0, STAGE_COLS)],
                                     stage_ref.at[i % 2, pl.ds(0, rows), :], sem_ref.at[i % 2])

    copy(0).start()
    for i, (_, dst, r0, rows, c0) in enumerate(chunks):
        if i + 1 < len(chunks):
            copy(i + 1).start()
        copy(i).wait()
        dst[r0:r0 + rows, c0:c0 + STAGE_COLS] = stage_ref[i % 2, 0:rows, :].astype(BF16)


def _even_in_kernel(layer, x_ref, w_hbm, rope_ref, o0_ref, o1_ref, o2_ref, glu_ref, slab_ref, w_ref,
                    stage_ref, sem_ref):
    @pl.when((pl.program_id(0) == 0) & (pl.program_id(1) == 0))
    def _():
        _fetch_weights_bf16([(w_hbm.at[layer], w_ref)], stage_ref, sem_ref)

    tm = x_ref.shape[1]
    xb = x_ref[0].astype(BF16)
    outs = (o0_ref, o1_ref, o2_ref)
    for j in range(3):
        for g, dil in enumerate(DILATIONS):
            c0 = j * QK_WIDTH + g * GROUP_WIDTH
            acc = _dot(xb, w_ref[:, c0:c0 + GROUP_WIDTH])
            slab, regroup = slab_ref.at[0], slab_ref.at[1]
            for h in range(HEADS):
                t = acc[:, h * HEAD_DIM:(h + 1) * HEAD_DIM]
                if j < 2:
                    t = t * rope_ref[2 * j] + pltpu.roll(t, HEAD_DIM // 2, 1) * rope_ref[2 * j + 1]
                if dil == 1:
                    o0_ref[0, j, h] = t.astype(BF16)
                else:
                    slab[h] = t
            if dil == SPLIT_STRIDE:
                for h in range(HEADS):
                    for r in range(dil):
                        outs[g][0, j, h, r] = slab[h, pl.ds(r, tm // dil, stride=dil), :].astype(BF16)
            elif dil == SPLIT_STRIDE * SPLIT_STRIDE:
                part = tm // SPLIT_STRIDE
                for h in range(HEADS):
                    for r in range(SPLIT_STRIDE):
                        regroup[h, r * part:(r + 1) * part, :] = slab[h, pl.ds(r, part, stride=SPLIT_STRIDE), :]
                for h in range(HEADS):
                    for r in range(dil):
                        lo, hi = r % SPLIT_STRIDE, r // SPLIT_STRIDE
                        rows = pl.ds(lo * part + hi, tm // dil, stride=SPLIT_STRIDE)
                        outs[g][0, j, h, r] = regroup[h, rows, :].astype(BF16)
    a = _dot(xb, w_ref[:, 3 * QK_WIDTH:3 * QK_WIDTH + CONF_WIDTH])
    gate = _dot(xb, w_ref[:, 3 * QK_WIDTH + CONF_WIDTH:])
    glu_ref[0] = a * _sigmoid(gate)


def _even_in(x, w_in, layer, rope_tab):
    b, s, d = x.shape
    tm = ROW_TILE
    qkv_shapes = [jax.ShapeDtypeStruct((b, 3, HEADS, s, HEAD_DIM), BF16)]
    qkv_specs = [pl.BlockSpec((1, 3, HEADS, tm, HEAD_DIM), lambda bi, mi: (bi, 0, 0, mi, 0))]
    for dil in DILATIONS[1:]:
        qkv_shapes.append(jax.ShapeDtypeStruct((b, 3, HEADS, dil, s // dil, HEAD_DIM), BF16))
        qkv_specs.append(pl.BlockSpec((1, 3, HEADS, dil, tm // dil, HEAD_DIM),
                                      lambda bi, mi: (bi, 0, 0, 0, mi, 0)))
    return pl.pallas_call(
        functools.partial(_even_in_kernel, layer),
        grid=(b, s // tm),
        in_specs=[
            pl.BlockSpec((1, tm, d), lambda bi, mi: (bi, mi, 0)),
            _in_hbm(),
            pl.BlockSpec((4, tm, HEAD_DIM), lambda bi, mi: (0, mi, 0)),
        ],
        out_specs=qkv_specs + [pl.BlockSpec((1, tm, CONF_WIDTH), lambda bi, mi: (bi, mi, 0))],
        out_shape=qkv_shapes + [jax.ShapeDtypeStruct((b, s, CONF_WIDTH), F32)],
        scratch_shapes=[pltpu.VMEM((2, HEADS, tm, HEAD_DIM), F32), _weight_scratch(w_in)] + _staging_scratch(),
        compiler_params=pltpu.CompilerParams(
            dimension_semantics=("arbitrary", "arbitrary"), vmem_limit_bytes=V7X_VMEM_LIMIT),
        name="even_in_proj",
    )(x, w_in, rope_tab)


ATT_BATCH = 8


def _attn_kernel(g0_ref, g1_ref, g2_ref, o_ref, kk_ref, vv_ref, out_ref, lse_ref, stage_ref):
    seq = o_ref.shape[1]
    nb = ATT_BLOCK
    n_blocks = seq // nb
    qi = lax.broadcasted_iota(jnp.int32, (nb, 1), 0)
    kj = lax.broadcasted_iota(jnp.int32, (1, 2 * nb), 1)
    cur_only = (kj >= nb) & (kj <= qi + nb)
    band = (kj >= qi) & (kj <= qi + nb)

    for g, ref, dil in zip(range(3), (g0_ref, g1_ref, g2_ref), DILATIONS):
        per_res = n_blocks // dil
        firsts = range(0, n_blocks, per_res)
        if per_res > 1:
            for j, win_ref in ((1, kk_ref), (2, vv_ref)):
                win_ref[:, nb:, :] = ref[0, j, 0]
                win_ref[1:, 0:nb, :] = ref[0, j, 0, 0:n_blocks - 1]
                for i in firsts:
                    win_ref[i, 0:nb, :] = jnp.zeros((nb, HEAD_DIM), BF16)
            keys, vals, lo = (lambda i: kk_ref[i]), (lambda i: vv_ref[i]), 0
        else:
            keys, vals, lo = (lambda i: ref[0, 1, 0, i]), (lambda i: ref[0, 2, 0, i]), nb

        for c0 in range(0, n_blocks, ATT_BATCH):
            blocks = list(range(c0, c0 + ATT_BATCH))
            s = [lax.dot_general(ref[0, 0, 0, i], keys(i), (((1,), (1,)), ((), ())),
                                 preferred_element_type=F32) for i in blocks]
            masks = [(cur_only if i in firsts else band)[:, lo:] for i in blocks]
            s = [jnp.where(mk, v, NEG_INF) for mk, v in zip(masks, s)]
            m = [jnp.max(v, axis=-1, keepdims=True) for v in s]
            ex = [jnp.exp2(v - mx) for v, mx in zip(s, m)]
            dens = [jnp.sum(e, axis=-1, keepdims=True) for e in ex]
            acc = [_dot(e.astype(BF16), vals(i)) for e, i in zip(ex, blocks)]
            for i, a, mx, den in zip(blocks, acc, m, dens):
                out = a[:, 0:nb] / den
                lse = jnp.broadcast_to(mx + jnp.log2(den), (nb, HEAD_DIM))
                start = (i % per_res) * nb * dil + i // per_res
                if dil == SPLIT_STRIDE * SPLIT_STRIDE:
                    part, row0 = start % SPLIT_STRIDE, start // SPLIT_STRIDE
                    rows = pl.ds(row0, nb, stride=SPLIT_STRIDE)
                    stage_ref[0, part, rows, :] = out
                    stage_ref[1, part, rows, :] = lse
                else:
                    rows = pl.ds(start, nb) if dil == 1 else pl.ds(start, nb, stride=dil)
                    out_ref[g, rows, :] = out
                    lse_ref[g, rows, :] = lse
        if dil == SPLIT_STRIDE * SPLIT_STRIDE:
            for part in range(SPLIT_STRIDE):
                rows = pl.ds(part, seq // SPLIT_STRIDE, stride=SPLIT_STRIDE)
                out_ref[g, rows, :] = stage_ref[0, part]
                lse_ref[g, rows, :] = stage_ref[1, part]

    chunk = 256
    for c in range(seq // chunk):
        rows = pl.ds(c * chunk, chunk)
        lse = [lse_ref[g, rows, :] for g in range(3)]
        top = jnp.maximum(jnp.maximum(lse[0], lse[1]), lse[2])
        num = jnp.zeros((chunk, HEAD_DIM), F32)
        den = jnp.zeros((chunk, HEAD_DIM), F32)
        for g in range(3):
            a = jnp.exp2(lse[g] - top)
            num = num + a * out_ref[g, rows, :]
            den = den + a
        o_ref[0, rows, :] = (num / den).astype(o_ref.dtype)


def _attention(g0, g1, g2):
    b, _, _, s, e = g0.shape
    nb = ATT_BLOCK
    blocked = [a.reshape(b, 3, HEADS, s // nb, nb, e) for a in (g0, g1, g2)]
    spec = pl.BlockSpec((1, 3, 1, s // nb, nb, e), lambda bi, hi: (bi, 0, hi, 0, 0, 0))
    return pl.pallas_call(
        _attn_kernel,
        grid=(b, HEADS),
        in_specs=[spec] * 3,
        out_specs=pl.BlockSpec((1, s, e), lambda bi, hi: (bi, 0, hi)),
        out_shape=jax.ShapeDtypeStruct((b, s, GROUP_WIDTH), BF16),
        scratch_shapes=[pltpu.VMEM((s // nb, 2 * nb, e), BF16)] * 2 + [pltpu.VMEM((3, s, e), F32)] * 2
        + [pltpu.VMEM((2, SPLIT_STRIDE, s // SPLIT_STRIDE, e), F32)],
        compiler_params=pltpu.CompilerParams(
            dimension_semantics=("arbitrary", "arbitrary"), vmem_limit_bytes=V7X_VMEM_LIMIT),
        name="dilated_attention",
    )(*blocked)


def _swiglu_ple(x1, p_rows, hid_ref, w_in_ref, w_out_ref, g_ref, b_ref, wp_ref, wg_ref, side_work=None):
    xb = x1.astype(BF16)
    for c in range(FFN_HIDDEN // FFN_CHUNK):
        cols = slice(c * FFN_CHUNK, (c + 1) * FFN_CHUNK)
        gate = _dot(xb, w_in_ref[:, cols])
        up = _dot(xb, w_in_ref[:, FFN_HIDDEN + c * FFN_CHUNK:FFN_HIDDEN + (c + 1) * FFN_CHUNK])
        hid_ref[:, cols] = (gate * _sigmoid(gate) * up).astype(BF16)
        if side_work is not None:
            side_work(c, gate[gate.shape[0] - SUBLANES:, 0:LANES])
    y = _dot(hid_ref[...], w_out_ref[...])
    x2 = _layer_norm(DN_ALPHA * x1 + y, g_ref[...], b_ref[...])
    emb = _dot(p_rows.astype(BF16), wp_ref[...])
    return x2 + emb * _sigmoid(_dot(x2.astype(BF16), wg_ref[...]))


def _ffn_specs(ffn_params, layer):
    w_in, w_out, ln_g, ln_b, w_proj, w_gate = ffn_params
    return [_in_hbm(), _in_hbm(), _layer_slab(ln_g, layer), _layer_slab(ln_b, layer), _in_hbm(), _in_hbm()]


def _ffn_scratch(ffn_params):
    w_in, w_out, _, _, w_proj, w_gate = ffn_params
    return [_weight_scratch(a) for a in (w_in, w_out, w_proj, w_gate)]


def _ffn_jobs(layer, ffn_in, ffn_bufs):
    w_in_hbm, w_out_hbm, g_ref, b_ref, wp_hbm, wg_hbm = ffn_in
    w_in_ref, w_out_ref, wp_ref, wg_ref = ffn_bufs
    jobs = [(w_in_hbm.at[layer], w_in_ref), (w_out_hbm.at[layer], w_out_ref),
            (wp_hbm.at[layer], wp_ref), (wg_hbm.at[layer], wg_ref)]
    return jobs, (w_in_ref, w_out_ref, g_ref, b_ref, wp_ref, wg_ref)


def _zero_from(anchor):
    bits = pltpu.bitcast(anchor, jnp.int32)
    return lax.shift_right_logical(lax.shift_right_logical(bits, 16), 16).astype(F32)[0:1, :]


def _conv_block(hist_ref, pre_ref, cw_ref, cb_ref, row_block, lane_block, after):
    first = CONV_PAD - (CONF_TAPS - 1)
    lanes = slice(lane_block * LANES, (lane_block + 1) * LANES)
    t0 = row_block * CONV_ROWS
    win = hist_ref[t0:t0 + CONV_ROWS + CONV_PAD + CONV_TAIL, lanes]
    for value in after:
        win = win + _zero_from(value)
    acc = jnp.broadcast_to(cb_ref[:, lanes], (CONV_ROWS, LANES))
    for s in range(SUBLANES):
        base, shift = divmod(first + s, SUBLANES)
        part = None
        for j in range(s, CONF_TAPS, SUBLANES):
            k0 = (j - s) + base * SUBLANES
            term = cw_ref[j:j + 1, lanes] * win[k0:k0 + CONV_ROWS + SUBLANES]
            part = term if part is None else part + term
        acc = acc + _shift_rows_up(part, shift)
    pre_ref[t0:t0 + CONV_ROWS, lanes] = acc
    return acc[0:SUBLANES]


def _conv_blocks(hist_ref, pre_ref, dst_ref, conv_refs, blocks, after):
    cw_ref, cb_ref, lg_ref, lb_ref = conv_refs
    for rb, lb in blocks:
        after = [_conv_block(hist_ref, pre_ref, cw_ref, cb_ref, rb, lb, after)]
        if lb == CONF_WIDTH // LANES - 1:
            rows = slice(rb * CONV_ROWS, (rb + 1) * CONV_ROWS)
            y = _layer_norm(pre_ref[rows, :], lg_ref[...], lb_ref[...])
            dst_ref[rows, :] = (y * _sigmoid(y)).astype(BF16)
    return after


def _even_tail_kernel(tiles_per_seq, mixer_layer, layer, attn_ref, glu0_ref, glu_next_ref, x_ref, p_ref, w_hbm,
                      mg_ref, mb_ref, cw_ref, cb_ref, lg_ref, lb_ref, *rest):
    ffn_in, rest = rest[:6], rest[6:]
    o_ref, hid_ref, hist_ref, pre_ref, conv_cur_ref, conv_next_ref, w_ref, *ffn_bufs, stage_ref, sem_ref = rest
    tm = x_ref.shape[0]
    step = pl.program_id(0)
    ffn_jobs, ffn_refs = _ffn_jobs(layer, ffn_in, ffn_bufs)

    @pl.when(step == 0)
    def _():
        _fetch_weights_bf16([(w_hbm.at[mixer_layer], w_ref)] + ffn_jobs, stage_ref, sem_ref)

    conv_refs = (cw_ref, cb_ref, lg_ref, lb_ref)
    blocks = [(rb, lb) for rb in range(tm // CONV_ROWS) for lb in range(CONF_WIDTH // LANES)]

    def stage(glu_ref, conv_tile):
        hist_ref[CONV_PAD:CONV_PAD + tm, :] = glu_ref[...]
        return (conv_tile + 1) % tiles_per_seq != 0

    def carry(keep):
        tail = hist_ref[tm:tm + CONV_PAD, :]
        hist_ref[0:CONV_PAD, :] = jnp.where(keep, tail, 0.0)

    @pl.when(step == 0)
    def _():
        hist_ref[...] = jnp.zeros(hist_ref.shape, F32)
        keep = stage(glu0_ref, step)
        zero = jnp.zeros((SUBLANES, LANES), F32)
        _conv_blocks(hist_ref, pre_ref, conv_cur_ref, conv_refs, blocks, [zero])
        carry(keep)

    mix = _dot(attn_ref[...], w_ref[0:GROUP_WIDTH, :]) + _dot(conv_cur_ref[...], w_ref[GROUP_WIDTH:, :])
    x1 = _layer_norm(DN_ALPHA * x_ref[...] + mix, mg_ref[...], mb_ref[...])

    keep = stage(glu_next_ref, step + 1)
    chunks = FFN_HIDDEN // FFN_CHUNK
    per_chunk = -(-len(blocks) // chunks)
    state = {"after": []}

    def side_work(c, anchor):
        todo = blocks[c * per_chunk:(c + 1) * per_chunk]
        state["after"] = _conv_blocks(hist_ref, pre_ref, conv_next_ref, conv_refs, todo, [anchor] + state["after"])

    o_ref[...] = _swiglu_ple(x1, p_ref[...], hid_ref, *ffn_refs, side_work=side_work)
    carry(keep)
    conv_cur_ref[...] = conv_next_ref[...]


def _even_tail(attn, glu, x2d, p_all, w_out, conv_params, mixer_layer, ln_g, ln_b, ffn_params, layer, tiles_per_seq):
    n, d = x2d.shape
    tm = ROW_TILE
    last = n // tm - 1
    return pl.pallas_call(
        functools.partial(_even_tail_kernel, tiles_per_seq, mixer_layer, layer),
        grid=(n // tm,),
        in_specs=[
            pl.BlockSpec((tm, GROUP_WIDTH), lambda i: (i, 0)),
            pl.BlockSpec((tm, CONF_WIDTH), lambda i: (0, 0)),
            pl.BlockSpec((tm, CONF_WIDTH), lambda i: (jnp.minimum(i + 1, last), 0)),
            pl.BlockSpec((tm, d), lambda i: (i, 0)),
            pl.BlockSpec((None, tm, PLE_DIM), lambda i: (layer, i, 0)),
            _in_hbm(), _layer_slab(ln_g, layer), _layer_slab(ln_b, layer),
        ] + [_layer_slab(a, mixer_layer) for a in conv_params] + _ffn_specs(ffn_params, layer),
        out_specs=pl.BlockSpec((tm, d), lambda i: (i, 0)),
        out_shape=jax.ShapeDtypeStruct((n, d), F32),
        scratch_shapes=[pltpu.VMEM((tm, FFN_HIDDEN), BF16),
                        pltpu.VMEM((CONV_PAD + tm + CONV_TAIL, CONF_WIDTH), F32),
                        pltpu.VMEM((tm, CONF_WIDTH), F32),
                        pltpu.VMEM((tm, CONF_WIDTH), BF16), pltpu.VMEM((tm, CONF_WIDTH), BF16),
                        _weight_scratch(w_out)] + _ffn_scratch(ffn_params) + _staging_scratch(),
        compiler_params=pltpu.CompilerParams(
            dimension_semantics=("arbitrary",), vmem_limit_bytes=V7X_VMEM_LIMIT),
        name="conformer_out_proj_swiglu_ple",
    )(attn, glu, glu, x2d, p_all, w_out, ln_g, ln_b, *conv_params, *ffn_params)


SHORT_CARRY = 8


def _odd_layer_kernel(mixer_layer, layer, x_ref, p_ref, w_in_hbm, cw_ref, w_out_hbm, mg_ref, mb_ref, *rest):
    ffn_in, rest = rest[:6], rest[6:]
    o_ref, gate_ref, mix_ref, hid_ref, w_in_ref, w_out_ref, *ffn_bufs, stage_ref, sem_ref = rest
    ffn_jobs, ffn_refs = _ffn_jobs(layer, ffn_in, ffn_bufs)

    @pl.when((pl.program_id(0) == 0) & (pl.program_id(1) == 0))
    def _():
        mixer_jobs = [(w_in_hbm.at[mixer_layer], w_in_ref), (w_out_hbm.at[mixer_layer], w_out_ref)]
        _fetch_weights_bf16(mixer_jobs + ffn_jobs, stage_ref, sem_ref)

    tm = x_ref.shape[1]
    width = D_MODEL
    chunk = 512

    @pl.when(pl.program_id(1) == 0)
    def _():
        gate_ref[0:SHORT_CARRY, :] = jnp.zeros((SHORT_CARRY, width), F32)

    xb = x_ref[0].astype(BF16)
    for c in range(width // chunk):
        cols = slice(c * chunk, (c + 1) * chunk)
        cg = _dot(xb, w_in_ref[:, width + c * chunk:width + (c + 1) * chunk])
        hh = _dot(xb, w_in_ref[:, 2 * width + c * chunk:2 * width + (c + 1) * chunk])
        gate_ref[SHORT_CARRY:, cols] = cg * hh
        y = jnp.zeros((tm, chunk), F32)
        for j in range(SHORT_TAPS):
            off = SHORT_CARRY - (SHORT_TAPS - 1) + j
            y = y + cw_ref[j:j + 1, cols] * gate_ref[off:off + tm, cols]
        bg = _dot(xb, w_in_ref[:, cols])
        mix_ref[:, cols] = (bg * y).astype(BF16)
        gate_ref[0:SHORT_CARRY, cols] = gate_ref[tm:tm + SHORT_CARRY, cols]
    mix = _dot(mix_ref[...], w_out_ref[...])
    x1 = _layer_norm(DN_ALPHA * x_ref[0] + mix, mg_ref[...], mb_ref[...])
    o_ref[0] = _swiglu_ple(x1, p_ref[...], hid_ref, *ffn_refs)


def _odd_layer(x, p_all, w_in, conv_w, w_out, mixer_layer, ln_g, ln_b, ffn_params, layer):
    b, s, d = x.shape
    tm = ROW_TILE
    tiles = s // tm
    return pl.pallas_call(
        functools.partial(_odd_layer_kernel, mixer_layer, layer),
        grid=(b, tiles),
        in_specs=[
            pl.BlockSpec((1, tm, d), lambda bi, mi: (bi, mi, 0)),
            pl.BlockSpec((None, tm, PLE_DIM), lambda bi, mi: (layer, bi * tiles + mi, 0)),
            _in_hbm(), _layer_slab(conv_w, mixer_layer), _in_hbm(),
            _layer_slab(ln_g, layer), _layer_slab(ln_b, layer),
        ] + _ffn_specs(ffn_params, layer),
        out_specs=pl.BlockSpec((1, tm, d), lambda bi, mi: (bi, mi, 0)),
        out_shape=jax.ShapeDtypeStruct((b, s, d), F32),
        scratch_shapes=[pltpu.VMEM((SHORT_CARRY + tm, d), F32), pltpu.VMEM((tm, d), BF16),
                        pltpu.VMEM((tm, FFN_HIDDEN), BF16), _weight_scratch(w_in), _weight_scratch(w_out)]
        + _ffn_scratch(ffn_params) + _staging_scratch(),
        compiler_params=pltpu.CompilerParams(
            dimension_semantics=("arbitrary", "arbitrary"), vmem_limit_bytes=V7X_VMEM_LIMIT),
        name="short_conv_mixer_swiglu_ple",
    )(x, p_all, w_in, conv_w, w_out, ln_g, ln_b, *ffn_params)


def _rope_tables(seq):
    half = HEAD_DIM // 2
    inv = ROPE_THETA ** (-np.arange(half, dtype=np.float64) / half)
    ang = np.arange(seq, dtype=np.float64)[:, None] * inv[None, :]
    cos = np.concatenate([np.cos(ang), np.cos(ang)], axis=-1)
    sin = np.concatenate([-np.sin(ang), np.sin(ang)], axis=-1)
    scale = HEAD_DIM ** -0.5 * np.log2(np.e)
    return np.stack([cos * scale, sin * scale, cos, sin]).astype(np.float32)


def kernel(x, p, even_w_in, even_w_out, conf_conv_w, conf_conv_b, conf_ln_g, conf_ln_b, odd_w_in, odd_conv_w, odd_w_out, ln_mix_g, ln_mix_b, ln_ffn_g, ln_ffn_b, ffn_w_in, ffn_w_out, ple_w_proj, ple_w_gate):
    b, s, d = x.shape
    depth = p.shape[0]
    rope_tab = _rope_tables(s)
    p_all = p.reshape(depth, b * s, PLE_DIM)
    conf_conv_b, conf_ln_g, conf_ln_b, ln_mix_g, ln_mix_b, ln_ffn_g, ln_ffn_b = (
        _rows(v) for v in (conf_conv_b, conf_ln_g, conf_ln_b, ln_mix_g, ln_mix_b, ln_ffn_g, ln_ffn_b))
    for i in range(depth):
        j = i // 2
        ffn_params = (ffn_w_in, ffn_w_out, ln_ffn_g, ln_ffn_b, ple_w_proj, ple_w_gate)
        if i % 2 == 0:
            g0, g1, g2, glu = _even_in(x, even_w_in, j, rope_tab)
            attn = _attention(g0, g1, g2)
            conv_params = (conf_conv_w, conf_conv_b, conf_ln_g, conf_ln_b)
            x = _even_tail(attn.reshape(b * s, -1), glu.reshape(b * s, -1), x.reshape(b * s, d), p_all,
                           even_w_out, conv_params, j, ln_mix_g, ln_mix_b, ffn_params, i,
                           s // ROW_TILE).reshape(b, s, d)
        else:
            x = _odd_layer(x, p_all, odd_w_in, odd_conv_w, odd_w_out, j, ln_mix_g, ln_mix_b, ffn_params, i)
    return x
```

```python
import functools

import jax
import jax.numpy as jnp
import numpy as np
from jax import lax
from jax.experimental import pallas as pl
from jax.experimental.pallas import tpu as pltpu

F32 = jnp.float32
BF16 = jnp.bfloat16

D_MODEL = 1024
HEADS = 4
HEAD_DIM = 128
DILATIONS = (1, 4, 16)
ATT_BLOCK = 128
GROUP_WIDTH = HEADS * HEAD_DIM
QK_WIDTH = len(DILATIONS) * GROUP_WIDTH
CONF_WIDTH = 512
CONF_TAPS = 31
SHORT_TAPS = 3
FFN_HIDDEN = 2816
PLE_DIM = 256
ROPE_THETA = 10000.0
NEG_INF = -1e30
LN_EPS = 1e-5
DN_ALPHA = float(4 ** 0.25)

SPLIT_STRIDE = 4
ROW_TILE = 512
FFN_CHUNK = 256
CONV_PAD = 32
CONV_ROWS = 64
SUBLANES = 8
LANES = 128
CONV_TAIL = SUBLANES
V7X_VMEM_LIMIT = 56 * 1024 * 1024


def _dot(a, b):
    return jnp.dot(a, b, preferred_element_type=F32)


def _layer_norm(v, g, b):
    mu = jnp.mean(v, axis=-1, keepdims=True)
    c = v - mu
    var = jnp.mean(c * c, axis=-1, keepdims=True)
    return c * lax.rsqrt(var + LN_EPS) * g + b


def _sigmoid(v):
    return 1.0 / (1.0 + jnp.exp(-v))


def _shift_rows_up(v, shift):
    rows = v.shape[0] - SUBLANES
    if shift == 0:
        return v[0:rows]
    tiles = [pltpu.roll(v[k:k + SUBLANES], SUBLANES - shift, 0) for k in range(0, rows + SUBLANES, SUBLANES)]
    low = lax.broadcasted_iota(jnp.int32, (SUBLANES, v.shape[1]), 0) < SUBLANES - shift
    return jnp.concatenate([jnp.where(low, a, b) for a, b in zip(tiles[:-1], tiles[1:])], axis=0)


def _resident(shape):
    return pl.BlockSpec(shape, lambda *_: (0,) * len(shape), pipeline_mode=pl.Buffered(1))


def _layer_slab(stacked, layer):
    return pl.BlockSpec((None,) + stacked.shape[1:], lambda *_: (layer, 0, 0), pipeline_mode=pl.Buffered(1))


def _rows(stacked):
    return stacked.reshape(stacked.shape[0], 1, stacked.shape[1])


STAGE_ROWS = 512
STAGE_COLS = 512
STAGE_SLOTS = 4


def _in_hbm():
    return pl.BlockSpec(memory_space=pl.ANY)


def _weight_scratch(stacked):
    return pltpu.VMEM(stacked.shape[1:], BF16)


def _staging_scratch():
    return [pltpu.VMEM((STAGE_SLOTS, STAGE_ROWS, STAGE_COLS), F32), pltpu.SemaphoreType.DMA((STAGE_SLOTS,))]


def _fetch_weights_bf16(jobs, stage_ref, sem_ref):
    ahead = STAGE_SLOTS - 1
    chunks = [(src, dst, r0, min(STAGE_ROWS, dst.shape[0] - r0), c0)
              for src, dst in jobs
              for c0 in range(0, dst.shape[1], STAGE_COLS)
              for r0 in range(0, dst.shape[0], STAGE_ROWS)]

    def copy(i):
        src, _, r0, rows, c0 = chunks[i]
        return pltpu.make_async_copy(src.at[pl.ds(r0, rows), pl.ds(c0, STAGE_COLS)],
                                     stage_ref.at[i % STAGE_SLOTS, pl.ds(0, rows), :], sem_ref.at[i % STAGE_SLOTS])

    for i in range(min(ahead, len(chunks))):
        copy(i).start()
    for i, (_, dst, r0, rows, c0) in enumerate(chunks):
        if i + ahead < len(chunks):
            copy(i + ahead).start()
        copy(i).wait()
        dst[r0:r0 + rows, c0:c0 + STAGE_COLS] = stage_ref[i % STAGE_SLOTS, 0:rows, :].astype(BF16)


def _even_in_kernel(layer, x_ref, w_hbm, rope_ref, o0_ref, o1_ref, o2_ref, glu_ref, slab_ref, w_ref,
                    stage_ref, sem_ref):
    @pl.when((pl.program_id(0) == 0) & (pl.program_id(1) == 0))
    def _():
        _fetch_weights_bf16([(w_hbm.at[layer], w_ref)], stage_ref, sem_ref)

    tm = x_ref.shape[1]
    xb = x_ref[0].astype(BF16)
    outs = (o0_ref, o1_ref, o2_ref)
    for j in range(3):
        for g, dil in enumerate(DILATIONS):
            c0 = j * QK_WIDTH + g * GROUP_WIDTH
            acc = _dot(xb, w_ref[:, c0:c0 + GROUP_WIDTH])
            slab, regroup = slab_ref.at[0], slab_ref.at[1]
            for h in range(HEADS):
                t = acc[:, h * HEAD_DIM:(h + 1) * HEAD_DIM]
                if j < 2:
                    t = t * rope_ref[2 * j] + pltpu.roll(t, HEAD_DIM // 2, 1) * rope_ref[2 * j + 1]
                if dil == 1:
                    o0_ref[0, j, h] = t.astype(BF16)
                else:
                    slab[h] = t
            if dil == SPLIT_STRIDE:
                for h in range(HEADS):
                    for r in range(dil):
                        outs[g][0, j, h, r] = slab[h, pl.ds(r, tm // dil, stride=dil), :].astype(BF16)
            elif dil == SPLIT_STRIDE * SPLIT_STRIDE:
                part = tm // SPLIT_STRIDE
                for h in range(HEADS):
                    for r in range(SPLIT_STRIDE):
                        regroup[h, r * part:(r + 1) * part, :] = slab[h, pl.ds(r, part, stride=SPLIT_STRIDE), :]
                for h in range(HEADS):
                    for r in range(dil):
                        lo, hi = r % SPLIT_STRIDE, r // SPLIT_STRIDE
                        rows = pl.ds(lo * part + hi, tm // dil, stride=SPLIT_STRIDE)
                        outs[g][0, j, h, r] = regroup[h, rows, :].astype(BF16)
    a = _dot(xb, w_ref[:, 3 * QK_WIDTH:3 * QK_WIDTH + CONF_WIDTH])
    gate = _dot(xb, w_ref[:, 3 * QK_WIDTH + CONF_WIDTH:])
    glu_ref[0] = a * _sigmoid(gate)


def _even_in(x, w_in, layer, rope_tab):
    b, s, d = x.shape
    tm = ROW_TILE
    qkv_shapes = [jax.ShapeDtypeStruct((b, 3, HEADS, s, HEAD_DIM), BF16)]
    qkv_specs = [pl.BlockSpec((1, 3, HEADS, tm, HEAD_DIM), lambda bi, mi: (bi, 0, 0, mi, 0))]
    for dil in DILATIONS[1:]:
        qkv_shapes.append(jax.ShapeDtypeStruct((b, 3, HEADS, dil, s // dil, HEAD_DIM), BF16))
        qkv_specs.append(pl.BlockSpec((1, 3, HEADS, dil, tm // dil, HEAD_DIM),
                                      lambda bi, mi: (bi, 0, 0, 0, mi, 0)))
    return pl.pallas_call(
        functools.partial(_even_in_kernel, layer),
        grid=(b, s // tm),
        in_specs=[
            pl.BlockSpec((1, tm, d), lambda bi, mi: (bi, mi, 0)),
            _in_hbm(),
            pl.BlockSpec((4, tm, HEAD_DIM), lambda bi, mi: (0, mi, 0)),
        ],
        out_specs=qkv_specs + [pl.BlockSpec((1, tm, CONF_WIDTH), lambda bi, mi: (bi, mi, 0))],
        out_shape=qkv_shapes + [jax.ShapeDtypeStruct((b, s, CONF_WIDTH), F32)],
        scratch_shapes=[pltpu.VMEM((2, HEADS, tm, HEAD_DIM), F32), _weight_scratch(w_in)] + _staging_scratch(),
        compiler_params=pltpu.CompilerParams(
            dimension_semantics=("arbitrary", "arbitrary"), vmem_limit_bytes=V7X_VMEM_LIMIT),
        name="even_in_proj",
    )(x, w_in, rope_tab)


ATT_BATCH = 8


def _attn_kernel(g0_ref, g1_ref, g2_ref, o_ref, kk_ref, vv_ref, out_ref, lse_ref, stage_ref):
    seq = o_ref.shape[1]
    nb = ATT_BLOCK
    n_blocks = seq // nb
    qi = lax.broadcasted_iota(jnp.int32, (nb, 1), 0)
    kj = lax.broadcasted_iota(jnp.int32, (1, 2 * nb), 1)
    cur_only = (kj >= nb) & (kj <= qi + nb)
    band = (kj >= qi) & (kj <= qi + nb)

    for g, ref, dil in zip(range(3), (g0_ref, g1_ref, g2_ref), DILATIONS):
        per_res = n_blocks // dil
        firsts = range(0, n_blocks, per_res)
        if per_res > 1:
            for j, win_ref in ((1, kk_ref), (2, vv_ref)):
                win_ref[:, nb:, :] = ref[0, j, 0]
                win_ref[1:, 0:nb, :] = ref[0, j, 0, 0:n_blocks - 1]
                for i in firsts:
                    win_ref[i, 0:nb, :] = jnp.zeros((nb, HEAD_DIM), BF16)
            keys, vals, lo = (lambda i: kk_ref[i]), (lambda i: vv_ref[i]), 0
        else:
            keys, vals, lo = (lambda i: ref[0, 1, 0, i]), (lambda i: ref[0, 2, 0, i]), nb

        for c0 in range(0, n_blocks, ATT_BATCH):
            blocks = list(range(c0, c0 + ATT_BATCH))
            s = [lax.dot_general(ref[0, 0, 0, i], keys(i), (((1,), (1,)), ((), ())),
                                 preferred_element_type=F32) for i in blocks]
            masks = [(cur_only if i in firsts else band)[:, lo:] for i in blocks]
            s = [jnp.where(mk, v, NEG_INF) for mk, v in zip(masks, s)]
            m = [jnp.max(v, axis=-1, keepdims=True) for v in s]
            ex = [jnp.exp2(v - mx) for v, mx in zip(s, m)]
            dens = [jnp.sum(e, axis=-1, keepdims=True) for e in ex]
            acc = [_dot(e.astype(BF16), vals(i)) for e, i in zip(ex, blocks)]
            for i, a, mx, den in zip(blocks, acc, m, dens):
                out = a[:, 0:nb] / den
                lse = jnp.broadcast_to(mx + jnp.log2(den), (nb, HEAD_DIM))
                start = (i % per_res) * nb * dil + i // per_res
                if dil == SPLIT_STRIDE * SPLIT_STRIDE:
                    part, row0 = start % SPLIT_STRIDE, start // SPLIT_STRIDE
                    rows = pl.ds(row0, nb, stride=SPLIT_STRIDE)
                    stage_ref[0, part, rows, :] = out
                    stage_ref[1, part, rows, :] = lse
                else:
                    rows = pl.ds(start, nb) if dil == 1 else pl.ds(start, nb, stride=dil)
                    out_ref[g, rows, :] = out
                    lse_ref[g, rows, :] = lse
        if dil == SPLIT_STRIDE * SPLIT_STRIDE:
            for part in range(SPLIT_STRIDE):
                rows = pl.ds(part, seq // SPLIT_STRIDE, stride=SPLIT_STRIDE)
                out_ref[g, rows, :] = stage_ref[0, part]
                lse_ref[g, rows, :] = stage_ref[1, part]

    chunk = 256
    for c in range(seq // chunk):
        rows = pl.ds(c * chunk, chunk)
        lse = [lse_ref[g, rows, :] for g in range(3)]
        top = jnp.maximum(jnp.maximum(lse[0], lse[1]), lse[2])
        num = jnp.zeros((chunk, HEAD_DIM), F32)
        den = jnp.zeros((chunk, HEAD_DIM), F32)
        for g in range(3):
            a = jnp.exp2(lse[g] - top)
            num = num + a * out_ref[g, rows, :]
            den = den + a
        o_ref[0, rows, :] = (num / den).astype(o_ref.dtype)


def _attention(g0, g1, g2):
    b, _, _, s, e = g0.shape
    nb = ATT_BLOCK
    blocked = [a.reshape(b, 3, HEADS, s // nb, nb, e) for a in (g0, g1, g2)]
    spec = pl.BlockSpec((1, 3, 1, s // nb, nb, e), lambda bi, hi: (bi, 0, hi, 0, 0, 0))
    return pl.pallas_call(
        _attn_kernel,
        grid=(b, HEADS),
        in_specs=[spec] * 3,
        out_specs=pl.BlockSpec((1, s, e), lambda bi, hi: (bi, 0, hi)),
        out_shape=jax.ShapeDtypeStruct((b, s, GROUP_WIDTH), BF16),
        scratch_shapes=[pltpu.VMEM((s // nb, 2 * nb, e), BF16)] * 2 + [pltpu.VMEM((3, s, e), F32)] * 2
        + [pltpu.VMEM((2, SPLIT_STRIDE, s // SPLIT_STRIDE, e), F32)],
        compiler_params=pltpu.CompilerParams(
            dimension_semantics=("arbitrary", "arbitrary"), vmem_limit_bytes=V7X_VMEM_LIMIT),
        name="dilated_attention",
    )(*blocked)


def _swiglu_ple(x1, p_rows, hid_ref, w_in_ref, w_out_ref, g_ref, b_ref, wp_ref, wg_ref, side_work=None):
    xb = x1.astype(BF16)
    for c in range(FFN_HIDDEN // FFN_CHUNK):
        cols = slice(c * FFN_CHUNK, (c + 1) * FFN_CHUNK)
        gate = _dot(xb, w_in_ref[:, cols])
        up = _dot(xb, w_in_ref[:, FFN_HIDDEN + c * FFN_CHUNK:FFN_HIDDEN + (c + 1) * FFN_CHUNK])
        hid_ref[:, cols] = (gate * _sigmoid(gate) * up).astype(BF16)
        if side_work is not None:
            side_work(c, gate[gate.shape[0] - SUBLANES:, 0:LANES])
    y = _dot(hid_ref[...], w_out_ref[...])
    x2 = _layer_norm(DN_ALPHA * x1 + y, g_ref[...], b_ref[...])
    emb = _dot(p_rows.astype(BF16), wp_ref[...])
    return x2 + emb * _sigmoid(_dot(x2.astype(BF16), wg_ref[...]))


def _ffn_specs(ffn_params, layer):
    w_in, w_out, ln_g, ln_b, w_proj, w_gate = ffn_params
    return [_in_hbm(), _in_hbm(), _layer_slab(ln_g, layer), _layer_slab(ln_b, layer), _in_hbm(), _in_hbm()]


def _ffn_scratch(ffn_params):
    w_in, w_out, _, _, w_proj, w_gate = ffn_params
    return [_weight_scratch(a) for a in (w_in, w_out, w_proj, w_gate)]


def _ffn_jobs(layer, ffn_in, ffn_bufs):
    w_in_hbm, w_out_hbm, g_ref, b_ref, wp_hbm, wg_hbm = ffn_in
    w_in_ref, w_out_ref, wp_ref, wg_ref = ffn_bufs
    jobs = [(w_in_hbm.at[layer], w_in_ref), (w_out_hbm.at[layer], w_out_ref),
            (wp_hbm.at[layer], wp_ref), (wg_hbm.at[layer], wg_ref)]
    return jobs, (w_in_ref, w_out_ref, g_ref, b_ref, wp_ref, wg_ref)


def _zero_from(anchor):
    bits = pltpu.bitcast(anchor, jnp.int32)
    return lax.shift_right_logical(lax.shift_right_logical(bits, 16), 16).astype(F32)[0:1, :]


def _conv_block(hist_ref, pre_ref, cw_ref, cb_ref, row_block, lane_block, after):
    first = CONV_PAD - (CONF_TAPS - 1)
    lanes = slice(lane_block * LANES, (lane_block + 1) * LANES)
    t0 = row_block * CONV_ROWS
    win = hist_ref[t0:t0 + CONV_ROWS + CONV_PAD + CONV_TAIL, lanes]
    for value in after:
        win = win + _zero_from(value)
    acc = jnp.broadcast_to(cb_ref[:, lanes], (CONV_ROWS, LANES))
    for s in range(SUBLANES):
        base, shift = divmod(first + s, SUBLANES)
        part = None
        for j in range(s, CONF_TAPS, SUBLANES):
            k0 = (j - s) + base * SUBLANES
            term = cw_ref[j:j + 1, lanes] * win[k0:k0 + CONV_ROWS + SUBLANES]
            part = term if part is None else part + term
        acc = acc + _shift_rows_up(part, shift)
    pre_ref[t0:t0 + CONV_ROWS, lanes] = acc
    return acc[0:SUBLANES]


def _conv_blocks(hist_ref, pre_ref, dst_ref, conv_refs, blocks, after):
    cw_ref, cb_ref, lg_ref, lb_ref = conv_refs
    for rb, lb in blocks:
        after = [_conv_block(hist_ref, pre_ref, cw_ref, cb_ref, rb, lb, after)]
        if lb == CONF_WIDTH // LANES - 1:
            rows = slice(rb * CONV_ROWS, (rb + 1) * CONV_ROWS)
            y = _layer_norm(pre_ref[rows, :], lg_ref[...], lb_ref[...])
            dst_ref[rows, :] = (y * _sigmoid(y)).astype(BF16)
    return after


def _even_tail_kernel(tiles_per_seq, mixer_layer, layer, attn_ref, glu0_ref, glu_next_ref, x_ref, p_ref, w_hbm,
                      mg_ref, mb_ref, cw_ref, cb_ref, lg_ref, lb_ref, *rest):
    ffn_in, rest = rest[:6], rest[6:]
    o_ref, hid_ref, hist_ref, pre_ref, conv_cur_ref, conv_next_ref, w_ref, *ffn_bufs, stage_ref, sem_ref = rest
    tm = x_ref.shape[0]
    step = pl.program_id(0)
    ffn_jobs, ffn_refs = _ffn_jobs(layer, ffn_in, ffn_bufs)

    @pl.when(step == 0)
    def _():
        _fetch_weights_bf16([(w_hbm.at[mixer_layer], w_ref)] + ffn_jobs, stage_ref, sem_ref)

    conv_refs = (cw_ref, cb_ref, lg_ref, lb_ref)
    blocks = [(rb, lb) for rb in range(tm // CONV_ROWS) for lb in range(CONF_WIDTH // LANES)]

    def stage(glu_ref, conv_tile):
        hist_ref[CONV_PAD:CONV_PAD + tm, :] = glu_ref[...]
        return (conv_tile + 1) % tiles_per_seq != 0

    def carry(keep):
        tail = hist_ref[tm:tm + CONV_PAD, :]
        hist_ref[0:CONV_PAD, :] = jnp.where(keep, tail, 0.0)

    @pl.when(step == 0)
    def _():
        hist_ref[...] = jnp.zeros(hist_ref.shape, F32)
        keep = stage(glu0_ref, step)
        zero = jnp.zeros((SUBLANES, LANES), F32)
        _conv_blocks(hist_ref, pre_ref, conv_cur_ref, conv_refs, blocks, [zero])
        carry(keep)

    mix = _dot(attn_ref[...], w_ref[0:GROUP_WIDTH, :]) + _dot(conv_cur_ref[...], w_ref[GROUP_WIDTH:, :])
    x1 = _layer_norm(DN_ALPHA * x_ref[...] + mix, mg_ref[...], mb_ref[...])

    keep = stage(glu_next_ref, step + 1)
    chunks = FFN_HIDDEN // FFN_CHUNK
    per_chunk = -(-len(blocks) // chunks)
    state = {"after": []}

    def side_work(c, anchor):
        todo = blocks[c * per_chunk:(c + 1) * per_chunk]
        state["after"] = _conv_blocks(hist_ref, pre_ref, conv_next_ref, conv_refs, todo, [anchor] + state["after"])

    o_ref[...] = _swiglu_ple(x1, p_ref[...], hid_ref, *ffn_refs, side_work=side_work)
    carry(keep)
    conv_cur_ref[...] = conv_next_ref[...]


def _even_tail(attn, glu, x2d, p_all, w_out, conv_params, mixer_layer, ln_g, ln_b, ffn_params, layer, tiles_per_seq):
    n, d = x2d.shape
    tm = ROW_TILE
    last = n // tm - 1
    return pl.pallas_call(
        functools.partial(_even_tail_kernel, tiles_per_seq, mixer_layer, layer),
        grid=(n // tm,),
        in_specs=[
            pl.BlockSpec((tm, GROUP_WIDTH), lambda i: (i, 0)),
            pl.BlockSpec((tm, CONF_WIDTH), lambda i: (0, 0)),
            pl.BlockSpec((tm, CONF_WIDTH), lambda i: (jnp.minimum(i + 1, last), 0)),
            pl.BlockSpec((tm, d), lambda i: (i, 0)),
            pl.BlockSpec((None, tm, PLE_DIM), lambda i: (layer, i, 0)),
            _in_hbm(), _layer_slab(ln_g, layer), _layer_slab(ln_b, layer),
        ] + [_layer_slab(a, mixer_layer) for a in conv_params] + _ffn_specs(ffn_params, layer),
        out_specs=pl.BlockSpec((tm, d), lambda i: (i, 0)),
        out_shape=jax.ShapeDtypeStruct((n, d), F32),
        scratch_shapes=[pltpu.VMEM((tm, FFN_HIDDEN), BF16),
                        pltpu.VMEM((CONV_PAD + tm + CONV_TAIL, CONF_WIDTH), F32),
                        pltpu.VMEM((tm, CONF_WIDTH), F32),
                        pltpu.VMEM((tm, CONF_WIDTH), BF16), pltpu.VMEM((tm, CONF_WIDTH), BF16),
                        _weight_scratch(w_out)] + _ffn_scratch(ffn_params) + _staging_scratch(),
        compiler_params=pltpu.CompilerParams(
            dimension_semantics=("arbitrary",), vmem_limit_bytes=V7X_VMEM_LIMIT),
        name="conformer_out_proj_swiglu_ple",
    )(attn, glu, glu, x2d, p_all, w_out, ln_g, ln_b, *conv_params, *ffn_params)


SHORT_CARRY = 8


def _odd_layer_kernel(mixer_layer, layer, x_ref, p_ref, w_in_hbm, cw_ref, w_out_hbm, mg_ref, mb_ref, *rest):
    ffn_in, rest = rest[:6], rest[6:]
    o_ref, gate_ref, mix_ref, hid_ref, w_in_ref, w_out_ref, *ffn_bufs, stage_ref, sem_ref = rest
    ffn_jobs, ffn_refs = _ffn_jobs(layer, ffn_in, ffn_bufs)

    @pl.when((pl.program_id(0) == 0) & (pl.program_id(1) == 0))
    def _():
        mixer_jobs = [(w_in_hbm.at[mixer_layer], w_in_ref), (w_out_hbm.at[mixer_layer], w_out_ref)]
        _fetch_weights_bf16(mixer_jobs + ffn_jobs, stage_ref, sem_ref)

    tm = x_ref.shape[1]
    width = D_MODEL
    chunk = 512

    @pl.when(pl.program_id(1) == 0)
    def _():
        gate_ref[0:SHORT_CARRY, :] = jnp.zeros((SHORT_CARRY, width), F32)

    xb = x_ref[0].astype(BF16)
    for c in range(width // chunk):
        cols = slice(c * chunk, (c + 1) * chunk)
        cg = _dot(xb, w_in_ref[:, width + c * chunk:width + (c + 1) * chunk])
        hh = _dot(xb, w_in_ref[:, 2 * width + c * chunk:2 * width + (c + 1) * chunk])
        gate_ref[SHORT_CARRY:, cols] = cg * hh
        y = jnp.zeros((tm, chunk), F32)
        for j in range(SHORT_TAPS):
            off = SHORT_CARRY - (SHORT_TAPS - 1) + j
            y = y + cw_ref[j:j + 1, cols] * gate_ref[off:off + tm, cols]
        bg = _dot(xb, w_in_ref[:, cols])
        mix_ref[:, cols] = (bg * y).astype(BF16)
        gate_ref[0:SHORT_CARRY, cols] = gate_ref[tm:tm + SHORT_CARRY, cols]
    mix = _dot(mix_ref[...], w_out_ref[...])
    x1 = _layer_norm(DN_ALPHA * x_ref[0] + mix, mg_ref[...], mb_ref[...])
    o_ref[0] = _swiglu_ple(x1, p_ref[...], hid_ref, *ffn_refs)


def _odd_layer(x, p_all, w_in, conv_w, w_out, mixer_layer, ln_g, ln_b, ffn_params, layer):
    b, s, d = x.shape
    tm = ROW_TILE
    tiles = s // tm
    return pl.pallas_call(
        functools.partial(_odd_layer_kernel, mixer_layer, layer),
        grid=(b, tiles),
        in_specs=[
            pl.BlockSpec((1, tm, d), lambda bi, mi: (bi, mi, 0)),
            pl.BlockSpec((None, tm, PLE_DIM), lambda bi, mi: (layer, bi * tiles + mi, 0)),
            _in_hbm(), _layer_slab(conv_w, mixer_layer), _in_hbm(),
            _layer_slab(ln_g, layer), _layer_slab(ln_b, layer),
        ] + _ffn_specs(ffn_params, layer),
        out_specs=pl.BlockSpec((1, tm, d), lambda bi, mi: (bi, mi, 0)),
        out_shape=jax.ShapeDtypeStruct((b, s, d), F32),
        scratch_shapes=[pltpu.VMEM((SHORT_CARRY + tm, d), F32), pltpu.VMEM((tm, d), BF16),
                        pltpu.VMEM((tm, FFN_HIDDEN), BF16), _weight_scratch(w_in), _weight_scratch(w_out)]
        + _ffn_scratch(ffn_params) + _staging_scratch(),
        compiler_params=pltpu.CompilerParams(
            dimension_semantics=("arbitrary", "arbitrary"), vmem_limit_bytes=V7X_VMEM_LIMIT),
        name="short_conv_mixer_swiglu_ple",
    )(x, p_all, w_in, conv_w, w_out, ln_g, ln_b, *ffn_params)


def _rope_tables(seq):
    half = HEAD_DIM // 2
    inv = ROPE_THETA ** (-np.arange(half, dtype=np.float64) / half)
    ang = np.arange(seq, dtype=np.float64)[:, None] * inv[None, :]
    cos = np.concatenate([np.cos(ang), np.cos(ang)], axis=-1)
    sin = np.concatenate([-np.sin(ang), np.sin(ang)], axis=-1)
    scale = HEAD_DIM ** -0.5 * np.log2(np.e)
    return np.stack([cos * scale, sin * scale, cos, sin]).astype(np.float32)


def kernel(x, p, even_w_in, even_w_out, conf_conv_w, conf_conv_b, conf_ln_g, conf_ln_b, odd_w_in, odd_conv_w, odd_w_out, ln_mix_g, ln_mix_b, ln_ffn_g, ln_ffn_b, ffn_w_in, ffn_w_out, ple_w_proj, ple_w_gate):
    b, s, d = x.shape
    depth = p.shape[0]
    rope_tab = _rope_tables(s)
    p_all = p.reshape(depth, b * s, PLE_DIM)
    conf_conv_b, conf_ln_g, conf_ln_b, ln_mix_g, ln_mix_b, ln_ffn_g, ln_ffn_b = (
        _rows(v) for v in (conf_conv_b, conf_ln_g, conf_ln_b, ln_mix_g, ln_mix_b, ln_ffn_g, ln_ffn_b))
    for i in range(depth):
        j = i // 2
        ffn_params = (ffn_w_in, ffn_w_out, ln_ffn_g, ln_ffn_b, ple_w_proj, ple_w_gate)
        if i % 2 == 0:
            g0, g1, g2, glu = _even_in(x, even_w_in, j, rope_tab)
            attn = _attention(g0, g1, g2)
            conv_params = (conf_conv_w, conf_conv_b, conf_ln_g, conf_ln_b)
            x = _even_tail(attn.reshape(b * s, -1), glu.reshape(b * s, -1), x.reshape(b * s, d), p_all,
                           even_w_out, conv_params, j, ln_mix_g, ln_mix_b, ffn_params, i,
                           s // ROW_TILE).reshape(b, s, d)
        else:
            x = _odd_layer(x, p_all, odd_w_in, odd_conv_w, odd_w_out, j, ln_mix_g, ln_mix_b, ffn_params, i)
    return x
```

```python
import functools

import jax
import jax.numpy as jnp
import numpy as np
from jax import lax
from jax.experimental import pallas as pl
from jax.experimental.pallas import tpu as pltpu

F32 = jnp.float32
BF16 = jnp.bfloat16

D_MODEL = 1024
HEADS = 4
HEAD_DIM = 128
DILATIONS = (1, 4, 16)
ATT_BLOCK = 128
GROUP_WIDTH = HEADS * HEAD_DIM
QK_WIDTH = len(DILATIONS) * GROUP_WIDTH
CONF_WIDTH = 512
CONF_TAPS = 31
SHORT_TAPS = 3
FFN_HIDDEN = 2816
PLE_DIM = 256
ROPE_THETA = 10000.0
NEG_INF = -1e30
LN_EPS = 1e-5
DN_ALPHA = float(4 ** 0.25)

SPLIT_STRIDE = 4
ROW_TILE = 512
FFN_CHUNK = 256
CONV_PAD = 32
CONV_ROWS = 64
SUBLANES = 8
LANES = 128
CONV_TAIL = SUBLANES
V7X_VMEM_LIMIT = 56 * 1024 * 1024


def _dot(a, b):
    return jnp.dot(a, b, preferred_element_type=F32)


def _layer_norm(v, g, b):
    mu = jnp.mean(v, axis=-1, keepdims=True)
    c = v - mu
    var = jnp.mean(c * c, axis=-1, keepdims=True)
    return c * lax.rsqrt(var + LN_EPS) * g + b


def _sigmoid(v):
    return 1.0 / (1.0 + jnp.exp(-v))


def _shift_rows_up(v, shift):
    rows = v.shape[0] - SUBLANES
    if shift == 0:
        return v[0:rows]
    tiles = [pltpu.roll(v[k:k + SUBLANES], SUBLANES - shift, 0) for k in range(0, rows + SUBLANES, SUBLANES)]
    low = lax.broadcasted_iota(jnp.int32, (SUBLANES, v.shape[1]), 0) < SUBLANES - shift
    return jnp.concatenate([jnp.where(low, a, b) for a, b in zip(tiles[:-1], tiles[1:])], axis=0)


def _resident(shape):
    return pl.BlockSpec(shape, lambda *_: (0,) * len(shape), pipeline_mode=pl.Buffered(1))


def _layer_slab(stacked, layer):
    return pl.BlockSpec((None,) + stacked.shape[1:], lambda *_: (layer, 0, 0), pipeline_mode=pl.Buffered(1))


def _rows(stacked):
    return stacked.reshape(stacked.shape[0], 1, stacked.shape[1])


STAGE_ROWS = 512
STAGE_COLS = 512
STAGE_SLOTS = 4


def _in_hbm():
    return pl.BlockSpec(memory_space=pl.ANY)


def _weight_scratch(stacked):
    return pltpu.VMEM(stacked.shape[1:], BF16)


def _staging_scratch():
    return [pltpu.VMEM((STAGE_SLOTS, STAGE_ROWS, STAGE_COLS), F32), pltpu.SemaphoreType.DMA((STAGE_SLOTS,))]


def _fetch_weights_bf16(jobs, stage_ref, sem_ref):
    ahead = STAGE_SLOTS - 1
    chunks = [(src, dst, r0, min(STAGE_ROWS, dst.shape[0] - r0), c0)
              for src, dst in jobs
              for c0 in range(0, dst.shape[1], STAGE_COLS)
              for r0 in range(0, dst.shape[0], STAGE_ROWS)]

    def copy(i):
        src, _, r0, rows, c0 = chunks[i]
        return pltpu.make_async_copy(src.at[pl.ds(r0, rows), pl.ds(c0, STAGE_COLS)],
                                     stage_ref.at[i % STAGE_SLOTS, pl.ds(0, rows), :], sem_ref.at[i % STAGE_SLOTS])

    for i in range(min(ahead, len(chunks))):
        copy(i).start()
    for i, (_, dst, r0, rows, c0) in enumerate(chunks):
        if i + ahead < len(chunks):
            copy(i + ahead).start()
        copy(i).wait()
        dst[r0:r0 + rows, c0:c0 + STAGE_COLS] = stage_ref[i % STAGE_SLOTS, 0:rows, :].astype(BF16)


def _even_in_kernel(layer, x_ref, w_hbm, rope_ref, o0_ref, o1_ref, o2_ref, glu_ref, slab_ref, w_ref,
                    stage_ref, sem_ref):
    @pl.when((pl.program_id(0) == 0) & (pl.program_id(1) == 0))
    def _():
        _fetch_weights_bf16([(w_hbm.at[layer], w_ref)], stage_ref, sem_ref)

    tm = x_ref.shape[1]
    xb = x_ref[0].astype(BF16)
    outs = (o0_ref, o1_ref, o2_ref)
    a = _dot(xb, w_ref[:, 3 * QK_WIDTH:3 * QK_WIDTH + CONF_WIDTH])
    gate = _dot(xb, w_ref[:, 3 * QK_WIDTH + CONF_WIDTH:])
    glu_ref[0] = a * _sigmoid(gate)
    for j in range(3):
        for g, dil in reversed(list(enumerate(DILATIONS))):
            c0 = j * QK_WIDTH + g * GROUP_WIDTH
            acc = _dot(xb, w_ref[:, c0:c0 + GROUP_WIDTH])
            slab, regroup = slab_ref.at[0], slab_ref.at[1]
            for h in range(HEADS):
                t = acc[:, h * HEAD_DIM:(h + 1) * HEAD_DIM]
                if j < 2:
                    t = t * rope_ref[2 * j] + pltpu.roll(t, HEAD_DIM // 2, 1) * rope_ref[2 * j + 1]
                if dil == 1:
                    o0_ref[0, j, h] = t.astype(BF16)
                else:
                    slab[h] = t
            if dil == SPLIT_STRIDE:
                for h in range(HEADS):
                    for r in range(dil):
                        outs[g][0, j, h, r] = slab[h, pl.ds(r, tm // dil, stride=dil), :].astype(BF16)
            elif dil == SPLIT_STRIDE * SPLIT_STRIDE:
                part = tm // SPLIT_STRIDE
                for h in range(HEADS):
                    for r in range(SPLIT_STRIDE):
                        regroup[h, r * part:(r + 1) * part, :] = slab[h, pl.ds(r, part, stride=SPLIT_STRIDE), :]
                for h in range(HEADS):
                    for r in range(dil):
                        lo, hi = r % SPLIT_STRIDE, r // SPLIT_STRIDE
                        rows = pl.ds(lo * part + hi, tm // dil, stride=SPLIT_STRIDE)
                        outs[g][0, j, h, r] = regroup[h, rows, :].astype(BF16)


def _even_in(x, w_in, layer, rope_tab):
    b, s, d = x.shape
    tm = ROW_TILE
    qkv_shapes = [jax.ShapeDtypeStruct((b, 3, HEADS, s, HEAD_DIM), BF16)]
    qkv_specs = [pl.BlockSpec((1, 3, HEADS, tm, HEAD_DIM), lambda bi, mi: (bi, 0, 0, mi, 0))]
    for dil in DILATIONS[1:]:
        qkv_shapes.append(jax.ShapeDtypeStruct((b, 3, HEADS, dil, s // dil, HEAD_DIM), BF16))
        qkv_specs.append(pl.BlockSpec((1, 3, HEADS, dil, tm // dil, HEAD_DIM),
                                      lambda bi, mi: (bi, 0, 0, 0, mi, 0)))
    return pl.pallas_call(
        functools.partial(_even_in_kernel, layer),
        grid=(b, s // tm),
        in_specs=[
            pl.BlockSpec((1, tm, d), lambda bi, mi: (bi, mi, 0)),
            _in_hbm(),
            pl.BlockSpec((4, tm, HEAD_DIM), lambda bi, mi: (0, mi, 0)),
        ],
        out_specs=qkv_specs + [pl.BlockSpec((1, tm, CONF_WIDTH), lambda bi, mi: (bi, mi, 0))],
        out_shape=qkv_shapes + [jax.ShapeDtypeStruct((b, s, CONF_WIDTH), F32)],
        scratch_shapes=[pltpu.VMEM((2, HEADS, tm, HEAD_DIM), F32), _weight_scratch(w_in)] + _staging_scratch(),
        compiler_params=pltpu.CompilerParams(
            dimension_semantics=("arbitrary", "arbitrary"), vmem_limit_bytes=V7X_VMEM_LIMIT),
        name="even_in_proj",
    )(x, w_in, rope_tab)


ATT_HEADS = 2
ATT_BATCH = 8


def _attn_head(head, g0_ref, g1_ref, g2_ref, o_ref, kk_ref, vv_ref, out_ref, lse_ref, stage_ref):
    seq = o_ref.shape[1]
    nb = ATT_BLOCK
    n_blocks = seq // nb
    qi = lax.broadcasted_iota(jnp.int32, (nb, 1), 0)
    kj = lax.broadcasted_iota(jnp.int32, (1, 2 * nb), 1)
    cur_only = (kj >= nb) & (kj <= qi + nb)
    band = (kj >= qi) & (kj <= qi + nb)

    for g, ref, dil in zip(range(3), (g0_ref, g1_ref, g2_ref), DILATIONS):
        per_res = n_blocks // dil
        firsts = range(0, n_blocks, per_res)
        if per_res > 1:
            for j, win_ref in ((1, kk_ref), (2, vv_ref)):
                win_ref[:, nb:, :] = ref[0, j, head]
                win_ref[1:, 0:nb, :] = ref[0, j, head, 0:n_blocks - 1]
                for i in firsts:
                    win_ref[i, 0:nb, :] = jnp.zeros((nb, HEAD_DIM), BF16)
            keys, vals, lo = (lambda i: kk_ref[i]), (lambda i: vv_ref[i]), 0
        else:
            keys, vals, lo = (lambda i: ref[0, 1, head, i]), (lambda i: ref[0, 2, head, i]), nb

        for c0 in range(0, n_blocks, ATT_BATCH):
            blocks = list(range(c0, c0 + ATT_BATCH))
            s = [lax.dot_general(ref[0, 0, head, i], keys(i), (((1,), (1,)), ((), ())),
                                 preferred_element_type=F32) for i in blocks]
            masks = [(cur_only if i in firsts else band)[:, lo:] for i in blocks]
            s = [jnp.where(mk, v, NEG_INF) for mk, v in zip(masks, s)]
            m = [jnp.max(v, axis=-1, keepdims=True) for v in s]
            ex = [jnp.exp2(v - mx) for v, mx in zip(s, m)]
            dens = [jnp.sum(e, axis=-1, keepdims=True) for e in ex]
            acc = [_dot(e.astype(BF16), vals(i)) for e, i in zip(ex, blocks)]
            for i, a, mx, den in zip(blocks, acc, m, dens):
                out = a[:, 0:nb] / den
                lse = jnp.broadcast_to(mx + jnp.log2(den), (nb, HEAD_DIM))
                start = (i % per_res) * nb * dil + i // per_res
                if dil == SPLIT_STRIDE * SPLIT_STRIDE:
                    part, row0 = start % SPLIT_STRIDE, start // SPLIT_STRIDE
                    rows = pl.ds(row0, nb, stride=SPLIT_STRIDE)
                    stage_ref[0, part, rows, :] = out
                    stage_ref[1, part, rows, :] = lse
                else:
                    rows = pl.ds(start, nb) if dil == 1 else pl.ds(start, nb, stride=dil)
                    out_ref[g, rows, :] = out
                    lse_ref[g, rows, :] = lse
        if dil == SPLIT_STRIDE * SPLIT_STRIDE:
            for part in range(SPLIT_STRIDE):
                rows = pl.ds(part, seq // SPLIT_STRIDE, stride=SPLIT_STRIDE)
                out_ref[g, rows, :] = stage_ref[0, part]
                lse_ref[g, rows, :] = stage_ref[1, part]

    chunk = 256
    for c in range(seq // chunk):
        rows = pl.ds(c * chunk, chunk)
        lse = [lse_ref[g, rows, :] for g in range(3)]
        top = jnp.maximum(jnp.maximum(lse[0], lse[1]), lse[2])
        num = jnp.zeros((chunk, HEAD_DIM), F32)
        den = jnp.zeros((chunk, HEAD_DIM), F32)
        for g in range(3):
            a = jnp.exp2(lse[g] - top)
            num = num + a * out_ref[g, rows, :]
            den = den + a
        o_ref[0, rows, head * HEAD_DIM:(head + 1) * HEAD_DIM] = (num / den).astype(o_ref.dtype)


def _attn_kernel(g0_ref, g1_ref, g2_ref, o_ref, *scratch):
    for head in range(g0_ref.shape[2]):
        _attn_head(head, g0_ref, g1_ref, g2_ref, o_ref, *scratch)


def _attention(g0, g1, g2):
    b, _, _, s, e = g0.shape
    nb = ATT_BLOCK
    blocked = [a.reshape(b, 3, HEADS, s // nb, nb, e) for a in (g0, g1, g2)]
    spec = pl.BlockSpec((1, 3, ATT_HEADS, s // nb, nb, e), lambda bi, hi: (bi, 0, hi, 0, 0, 0))
    return pl.pallas_call(
        _attn_kernel,
        grid=(b, HEADS // ATT_HEADS),
        in_specs=[spec] * 3,
        out_specs=pl.BlockSpec((1, s, ATT_HEADS * e), lambda bi, hi: (bi, 0, hi)),
        out_shape=jax.ShapeDtypeStruct((b, s, GROUP_WIDTH), BF16),
        scratch_shapes=[pltpu.VMEM((s // nb, 2 * nb, e), BF16)] * 2 + [pltpu.VMEM((3, s, e), F32)] * 2
        + [pltpu.VMEM((2, SPLIT_STRIDE, s // SPLIT_STRIDE, e), F32)],
        compiler_params=pltpu.CompilerParams(
            dimension_semantics=("arbitrary", "arbitrary"), vmem_limit_bytes=V7X_VMEM_LIMIT),
        name="dilated_attention",
    )(*blocked)


def _swiglu_ple(x1, p_rows, hid_ref, w_in_ref, w_out_ref, g_ref, b_ref, wp_ref, wg_ref, side_work=None):
    xb = x1.astype(BF16)
    for c in range(FFN_HIDDEN // FFN_CHUNK):
        cols = slice(c * FFN_CHUNK, (c + 1) * FFN_CHUNK)
        gate = _dot(xb, w_in_ref[:, cols])
        up = _dot(xb, w_in_ref[:, FFN_HIDDEN + c * FFN_CHUNK:FFN_HIDDEN + (c + 1) * FFN_CHUNK])
        hid_ref[:, cols] = (gate * _sigmoid(gate) * up).astype(BF16)
        if side_work is not None:
            side_work(c, gate[gate.shape[0] - SUBLANES:, 0:LANES])
    y = _dot(hid_ref[...], w_out_ref[...])
    x2 = _layer_norm(DN_ALPHA * x1 + y, g_ref[...], b_ref[...])
    emb = _dot(p_rows.astype(BF16), wp_ref[...])
    return x2 + emb * _sigmoid(_dot(x2.astype(BF16), wg_ref[...]))


def _ffn_specs(ffn_params, layer):
    w_in, w_out, ln_g, ln_b, w_proj, w_gate = ffn_params
    return [_in_hbm(), _in_hbm(), _layer_slab(ln_g, layer), _layer_slab(ln_b, layer), _in_hbm(), _in_hbm()]


def _ffn_scratch(ffn_params):
    w_in, w_out, _, _, w_proj, w_gate = ffn_params
    return [_weight_scratch(a) for a in (w_in, w_out, w_proj, w_gate)]


def _ffn_jobs(layer, ffn_in, ffn_bufs):
    w_in_hbm, w_out_hbm, g_ref, b_ref, wp_hbm, wg_hbm = ffn_in
    w_in_ref, w_out_ref, wp_ref, wg_ref = ffn_bufs
    jobs = [(w_in_hbm.at[layer], w_in_ref), (w_out_hbm.at[layer], w_out_ref),
            (wp_hbm.at[layer], wp_ref), (wg_hbm.at[layer], wg_ref)]
    return jobs, (w_in_ref, w_out_ref, g_ref, b_ref, wp_ref, wg_ref)


def _zero_from(anchor):
    bits = pltpu.bitcast(anchor, jnp.int32)
    return lax.shift_right_logical(lax.shift_right_logical(bits, 16), 16).astype(F32)[0:1, :]


def _conv_block(hist_ref, pre_ref, cw_ref, cb_ref, row_block, lane_block, after):
    first = CONV_PAD - (CONF_TAPS - 1)
    lanes = slice(lane_block * LANES, (lane_block + 1) * LANES)
    t0 = row_block * CONV_ROWS
    win = hist_ref[t0:t0 + CONV_ROWS + CONV_PAD + CONV_TAIL, lanes]
    for value in after:
        win = win + _zero_from(value)
    acc = jnp.broadcast_to(cb_ref[:, lanes], (CONV_ROWS, LANES))
    for s in range(SUBLANES):
        base, shift = divmod(first + s, SUBLANES)
        part = None
        for j in range(s, CONF_TAPS, SUBLANES):
            k0 = (j - s) + base * SUBLANES
            term = cw_ref[j:j + 1, lanes] * win[k0:k0 + CONV_ROWS + SUBLANES]
            part = term if part is None else part + term
        acc = acc + _shift_rows_up(part, shift)
    pre_ref[t0:t0 + CONV_ROWS, lanes] = acc
    return acc[0:SUBLANES]


def _conv_blocks(hist_ref, pre_ref, dst_ref, conv_refs, blocks, after):
    cw_ref, cb_ref, lg_ref, lb_ref = conv_refs
    for rb, lb in blocks:
        after = [_conv_block(hist_ref, pre_ref, cw_ref, cb_ref, rb, lb, after)]
        if lb == CONF_WIDTH // LANES - 1:
            rows = slice(rb * CONV_ROWS, (rb + 1) * CONV_ROWS)
            y = _layer_norm(pre_ref[rows, :], lg_ref[...], lb_ref[...])
            dst_ref[rows, :] = (y * _sigmoid(y)).astype(BF16)
    return after


def _even_tail_kernel(tiles_per_seq, mixer_layer, layer, attn_ref, glu0_ref, glu_next_ref, x_ref, p_ref, w_hbm,
                      mg_ref, mb_ref, cw_ref, cb_ref, lg_ref, lb_ref, *rest):
    ffn_in, rest = rest[:6], rest[6:]
    o_ref, hid_ref, hist_ref, pre_ref, conv_cur_ref, conv_next_ref, w_ref, *ffn_bufs, stage_ref, sem_ref = rest
    tm = x_ref.shape[0]
    step = pl.program_id(0)
    ffn_jobs, ffn_refs = _ffn_jobs(layer, ffn_in, ffn_bufs)

    @pl.when(step == 0)
    def _():
        _fetch_weights_bf16([(w_hbm.at[mixer_layer], w_ref)] + ffn_jobs, stage_ref, sem_ref)

    conv_refs = (cw_ref, cb_ref, lg_ref, lb_ref)
    blocks = [(rb, lb) for rb in range(tm // CONV_ROWS) for lb in range(CONF_WIDTH // LANES)]

    def stage(glu_ref, conv_tile):
        hist_ref[CONV_PAD:CONV_PAD + tm, :] = glu_ref[...]
        return (conv_tile + 1) % tiles_per_seq != 0

    def carry(keep):
        tail = hist_ref[tm:tm + CONV_PAD, :]
        hist_ref[0:CONV_PAD, :] = jnp.where(keep, tail, 0.0)

    @pl.when(step == 0)
    def _():
        hist_ref[...] = jnp.zeros(hist_ref.shape, F32)
        keep = stage(glu0_ref, step)
        zero = jnp.zeros((SUBLANES, LANES), F32)
        _conv_blocks(hist_ref, pre_ref, conv_cur_ref, conv_refs, blocks, [zero])
        carry(keep)

    mix = _dot(attn_ref[...], w_ref[0:GROUP_WIDTH, :]) + _dot(conv_cur_ref[...], w_ref[GROUP_WIDTH:, :])
    x1 = _layer_norm(DN_ALPHA * x_ref[...] + mix, mg_ref[...], mb_ref[...])

    keep = stage(glu_next_ref, step + 1)
    chunks = FFN_HIDDEN // FFN_CHUNK
    per_chunk = -(-len(blocks) // chunks)
    state = {"after": []}

    def side_work(c, anchor):
        todo = blocks[c * per_chunk:(c + 1) * per_chunk]
        state["after"] = _conv_blocks(hist_ref, pre_ref, conv_next_ref, conv_refs, todo, [anchor] + state["after"])

    o_ref[...] = _swiglu_ple(x1, p_ref[...], hid_ref, *ffn_refs, side_work=side_work)
    carry(keep)
    conv_cur_ref[...] = conv_next_ref[...]


def _even_tail(attn, glu, x2d, p_all, w_out, conv_params, mixer_layer, ln_g, ln_b, ffn_params, layer, tiles_per_seq):
    n, d = x2d.shape
    tm = ROW_TILE
    last = n // tm - 1
    return pl.pallas_call(
        functools.partial(_even_tail_kernel, tiles_per_seq, mixer_layer, layer),
        grid=(n // tm,),
        in_specs=[
            pl.BlockSpec((tm, GROUP_WIDTH), lambda i: (i, 0)),
            pl.BlockSpec((tm, CONF_WIDTH), lambda i: (0, 0)),
            pl.BlockSpec((tm, CONF_WIDTH), lambda i: (jnp.minimum(i + 1, last), 0)),
            pl.BlockSpec((tm, d), lambda i: (i, 0)),
            pl.BlockSpec((None, tm, PLE_DIM), lambda i: (layer, i, 0)),
            _in_hbm(), _layer_slab(ln_g, layer), _layer_slab(ln_b, layer),
        ] + [_layer_slab(a, mixer_layer) for a in conv_params] + _ffn_specs(ffn_params, layer),
        out_specs=pl.BlockSpec((tm, d), lambda i: (i, 0)),
        out_shape=jax.ShapeDtypeStruct((n, d), F32),
        scratch_shapes=[pltpu.VMEM((tm, FFN_HIDDEN), BF16),
                        pltpu.VMEM((CONV_PAD + tm + CONV_TAIL, CONF_WIDTH), F32),
                        pltpu.VMEM((tm, CONF_WIDTH), F32),
                        pltpu.VMEM((tm, CONF_WIDTH), BF16), pltpu.VMEM((tm, CONF_WIDTH), BF16),
                        _weight_scratch(w_out)] + _ffn_scratch(ffn_params) + _staging_scratch(),
        compiler_params=pltpu.CompilerParams(
            dimension_semantics=("arbitrary",), vmem_limit_bytes=V7X_VMEM_LIMIT),
        name="conformer_out_proj_swiglu_ple",
    )(attn, glu, glu, x2d, p_all, w_out, ln_g, ln_b, *conv_params, *ffn_params)


SHORT_CARRY = 8


def _odd_layer_kernel(mixer_layer, layer, x_ref, p_ref, w_in_hbm, cw_ref, w_out_hbm, mg_ref, mb_ref, *rest):
    ffn_in, rest = rest[:6], rest[6:]
    o_ref, gate_ref, mix_ref, hid_ref, w_in_ref, w_out_ref, *ffn_bufs, stage_ref, sem_ref = rest
    ffn_jobs, ffn_refs = _ffn_jobs(layer, ffn_in, ffn_bufs)

    @pl.when((pl.program_id(0) == 0) & (pl.program_id(1) == 0))
    def _():
        mixer_jobs = [(w_in_hbm.at[mixer_layer], w_in_ref), (w_out_hbm.at[mixer_layer], w_out_ref)]
        _fetch_weights_bf16(mixer_jobs + ffn_jobs, stage_ref, sem_ref)

    tm = x_ref.shape[1]
    width = D_MODEL
    chunk = 512

    @pl.when(pl.program_id(1) == 0)
    def _():
        gate_ref[0:SHORT_CARRY, :] = jnp.zeros((SHORT_CARRY, width), F32)

    xb = x_ref[0].astype(BF16)
    for c in range(width // chunk):
        cols = slice(c * chunk, (c + 1) * chunk)
        cg = _dot(xb, w_in_ref[:, width + c * chunk:width + (c + 1) * chunk])
        hh = _dot(xb, w_in_ref[:, 2 * width + c * chunk:2 * width + (c + 1) * chunk])
        gate_ref[SHORT_CARRY:, cols] = cg * hh
        y = jnp.zeros((tm, chunk), F32)
        for j in range(SHORT_TAPS):
            off = SHORT_CARRY - (SHORT_TAPS - 1) + j
            y = y + cw_ref[j:j + 1, cols] * gate_ref[off:off + tm, cols]
        bg = _dot(xb, w_in_ref[:, cols])
        mix_ref[:, cols] = (bg * y).astype(BF16)
        gate_ref[0:SHORT_CARRY, cols] = gate_ref[tm:tm + SHORT_CARRY, cols]
    mix = _dot(mix_ref[...], w_out_ref[...])
    x1 = _layer_norm(DN_ALPHA * x_ref[0] + mix, mg_ref[...], mb_ref[...])
    o_ref[0] = _swiglu_ple(x1, p_ref[...], hid_ref, *ffn_refs)


def _odd_layer(x, p_all, w_in, conv_w, w_out, mixer_layer, ln_g, ln_b, ffn_params, layer):
    b, s, d = x.shape
    tm = ROW_TILE
    tiles = s // tm
    return pl.pallas_call(
        functools.partial(_odd_layer_kernel, mixer_layer, layer),
        grid=(b, tiles),
        in_specs=[
            pl.BlockSpec((1, tm, d), lambda bi, mi: (bi, mi, 0)),
            pl.BlockSpec((None, tm, PLE_DIM), lambda bi, mi: (layer, bi * tiles + mi, 0)),
            _in_hbm(), _layer_slab(conv_w, mixer_layer), _in_hbm(),
            _layer_slab(ln_g, layer), _layer_slab(ln_b, layer),
        ] + _ffn_specs(ffn_params, layer),
        out_specs=pl.BlockSpec((1, tm, d), lambda bi, mi: (bi, mi, 0)),
        out_shape=jax.ShapeDtypeStruct((b, s, d), F32),
        scratch_shapes=[pltpu.VMEM((SHORT_CARRY + tm, d), F32), pltpu.VMEM((tm, d), BF16),
                        pltpu.VMEM((tm, FFN_HIDDEN), BF16), _weight_scratch(w_in), _weight_scratch(w_out)]
        + _ffn_scratch(ffn_params) + _staging_scratch(),
        compiler_params=pltpu.CompilerParams(
            dimension_semantics=("arbitrary", "arbitrary"), vmem_limit_bytes=V7X_VMEM_LIMIT),
        name="short_conv_mixer_swiglu_ple",
    )(x, p_all, w_in, conv_w, w_out, ln_g, ln_b, *ffn_params)


def _rope_tables(seq):
    half = HEAD_DIM // 2
    inv = ROPE_THETA ** (-np.arange(half, dtype=np.float64) / half)
    ang = np.arange(seq, dtype=np.float64)[:, None] * inv[None, :]
    cos = np.concatenate([np.cos(ang), np.cos(ang)], axis=-1)
    sin = np.concatenate([-np.sin(ang), np.sin(ang)], axis=-1)
    scale = HEAD_DIM ** -0.5 * np.log2(np.e)
    return np.stack([cos * scale, sin * scale, cos, sin]).astype(np.float32)


def kernel(x, p, even_w_in, even_w_out, conf_conv_w, conf_conv_b, conf_ln_g, conf_ln_b, odd_w_in, odd_conv_w, odd_w_out, ln_mix_g, ln_mix_b, ln_ffn_g, ln_ffn_b, ffn_w_in, ffn_w_out, ple_w_proj, ple_w_gate):
    b, s, d = x.shape
    depth = p.shape[0]
    rope_tab = _rope_tables(s)
    p_all = p.reshape(depth, b * s, PLE_DIM)
    conf_conv_b, conf_ln_g, conf_ln_b, ln_mix_g, ln_mix_b, ln_ffn_g, ln_ffn_b = (
        _rows(v) for v in (conf_conv_b, conf_ln_g, conf_ln_b, ln_mix_g, ln_mix_b, ln_ffn_g, ln_ffn_b))
    for i in range(depth):
        j = i // 2
        ffn_params = (ffn_w_in, ffn_w_out, ln_ffn_g, ln_ffn_b, ple_w_proj, ple_w_gate)
        if i % 2 == 0:
            g0, g1, g2, glu = _even_in(x, even_w_in, j, rope_tab)
            attn = _attention(g0, g1, g2)
            conv_params = (conf_conv_w, conf_conv_b, conf_ln_g, conf_ln_b)
            x = _even_tail(attn.reshape(b * s, -1), glu.reshape(b * s, -1), x.reshape(b * s, d), p_all,
                           even_w_out, conv_params, j, ln_mix_g, ln_mix_b, ffn_params, i,
                           s // ROW_TILE).reshape(b, s, d)
        else:
            x = _odd_layer(x, p_all, odd_w_in, odd_conv_w, odd_w_out, j, ln_mix_g, ln_mix_b, ffn_params, i)
    return x
```

```python
import functools

import jax
import jax.numpy as jnp
import numpy as np
from jax import lax
from jax.experimental import pallas as pl
from jax.experimental.pallas import tpu as pltpu

F32 = jnp.float32
BF16 = jnp.bfloat16

D_MODEL = 1024
HEADS = 4
HEAD_DIM = 128
DILATIONS = (1, 4, 16)
ATT_BLOCK = 128
GROUP_WIDTH = HEADS * HEAD_DIM
QK_WIDTH = len(DILATIONS) * GROUP_WIDTH
CONF_WIDTH = 512
CONF_TAPS = 31
SHORT_TAPS = 3
FFN_HIDDEN = 2816
PLE_DIM = 256
ROPE_THETA = 10000.0
NEG_INF = -1e30
LN_EPS = 1e-5
DN_ALPHA = float(4 ** 0.25)

SPLIT_STRIDE = 4
ROW_TILE = 512
FFN_CHUNK = 256
CONV_PAD = 32
CONV_ROWS = 64
SUBLANES = 8
LANES = 128
CONV_TAIL = SUBLANES
V7X_VMEM_LIMIT = 56 * 1024 * 1024


def _dot(a, b):
    return jnp.dot(a, b, preferred_element_type=F32)


def _layer_norm(v, g, b):
    mu = jnp.mean(v, axis=-1, keepdims=True)
    c = v - mu
    var = jnp.mean(c * c, axis=-1, keepdims=True)
    return c * lax.rsqrt(var + LN_EPS) * g + b


def _sigmoid(v):
    return 0.5 * jnp.tanh(0.5 * v) + 0.5


def _shift_rows_up(v, shift):
    rows = v.shape[0] - SUBLANES
    if shift == 0:
        return v[0:rows]
    tiles = [pltpu.roll(v[k:k + SUBLANES], SUBLANES - shift, 0) for k in range(0, rows + SUBLANES, SUBLANES)]
    low = lax.broadcasted_iota(jnp.int32, (SUBLANES, v.shape[1]), 0) < SUBLANES - shift
    return jnp.concatenate([jnp.where(low, a, b) for a, b in zip(tiles[:-1], tiles[1:])], axis=0)


def _resident(shape):
    return pl.BlockSpec(shape, lambda *_: (0,) * len(shape), pipeline_mode=pl.Buffered(1))


def _layer_slab(stacked, layer):
    return pl.BlockSpec((None,) + stacked.shape[1:], lambda *_: (layer, 0, 0), pipeline_mode=pl.Buffered(1))


def _rows(stacked):
    return stacked.reshape(stacked.shape[0], 1, stacked.shape[1])


STAGE_ROWS = 512
STAGE_COLS = 512
STAGE_SLOTS = 4


def _in_hbm():
    return pl.BlockSpec(memory_space=pl.ANY)


def _weight_scratch(stacked):
    return pltpu.VMEM(stacked.shape[1:], BF16)


def _staging_scratch():
    return [pltpu.VMEM((STAGE_SLOTS, STAGE_ROWS, STAGE_COLS), F32), pltpu.SemaphoreType.DMA((STAGE_SLOTS,))]


def _fetch_weights_bf16(jobs, stage_ref, sem_ref):
    ahead = STAGE_SLOTS - 1
    chunks = [(src, dst, r0, min(STAGE_ROWS, dst.shape[0] - r0), c0)
              for src, dst in jobs
              for c0 in range(0, dst.shape[1], STAGE_COLS)
              for r0 in range(0, dst.shape[0], STAGE_ROWS)]

    def copy(i):
        src, _, r0, rows, c0 = chunks[i]
        return pltpu.make_async_copy(src.at[pl.ds(r0, rows), pl.ds(c0, STAGE_COLS)],
                                     stage_ref.at[i % STAGE_SLOTS, pl.ds(0, rows), :], sem_ref.at[i % STAGE_SLOTS])

    for i in range(min(ahead, len(chunks))):
        copy(i).start()
    for i, (_, dst, r0, rows, c0) in enumerate(chunks):
        if i + ahead < len(chunks):
            copy(i + ahead).start()
        copy(i).wait()
        dst[r0:r0 + rows, c0:c0 + STAGE_COLS] = stage_ref[i % STAGE_SLOTS, 0:rows, :].astype(BF16)


def _even_in_kernel(layer, x_ref, w_hbm, rope_ref, o0_ref, o1_ref, o2_ref, glu_ref, slab_ref, w_ref,
                    stage_ref, sem_ref):
    @pl.when((pl.program_id(0) == 0) & (pl.program_id(1) == 0))
    def _():
        _fetch_weights_bf16([(w_hbm.at[layer], w_ref)], stage_ref, sem_ref)

    tm = x_ref.shape[1]
    xb = x_ref[0].astype(BF16)
    outs = (o0_ref, o1_ref, o2_ref)
    a = _dot(xb, w_ref[:, 3 * QK_WIDTH:3 * QK_WIDTH + CONF_WIDTH])
    gate = _dot(xb, w_ref[:, 3 * QK_WIDTH + CONF_WIDTH:])
    glu_ref[0] = a * _sigmoid(gate)
    for j in range(3):
        for g, dil in reversed(list(enumerate(DILATIONS))):
            c0 = j * QK_WIDTH + g * GROUP_WIDTH
            acc = _dot(xb, w_ref[:, c0:c0 + GROUP_WIDTH])
            slab, regroup = slab_ref.at[0], slab_ref.at[1]
            for h in range(HEADS):
                t = acc[:, h * HEAD_DIM:(h + 1) * HEAD_DIM]
                if j < 2:
                    t = t * rope_ref[2 * j] + pltpu.roll(t, HEAD_DIM // 2, 1) * rope_ref[2 * j + 1]
                if dil == 1:
                    o0_ref[0, j, h] = t.astype(BF16)
                else:
                    slab[h] = t
            if dil == SPLIT_STRIDE:
                for h in range(HEADS):
                    for r in range(dil):
                        outs[g][0, j, h, r] = slab[h, pl.ds(r, tm // dil, stride=dil), :].astype(BF16)
            elif dil == SPLIT_STRIDE * SPLIT_STRIDE:
                part = tm // SPLIT_STRIDE
                for h in range(HEADS):
                    for r in range(SPLIT_STRIDE):
                        regroup[h, r * part:(r + 1) * part, :] = slab[h, pl.ds(r, part, stride=SPLIT_STRIDE), :]
                for h in range(HEADS):
                    for r in range(dil):
                        lo, hi = r % SPLIT_STRIDE, r // SPLIT_STRIDE
                        rows = pl.ds(lo * part + hi, tm // dil, stride=SPLIT_STRIDE)
                        outs[g][0, j, h, r] = regroup[h, rows, :].astype(BF16)


def _even_in(x, w_in, layer, rope_tab):
    b, s, d = x.shape
    tm = ROW_TILE
    qkv_shapes = [jax.ShapeDtypeStruct((b, 3, HEADS, s, HEAD_DIM), BF16)]
    qkv_specs = [pl.BlockSpec((1, 3, HEADS, tm, HEAD_DIM), lambda bi, mi: (bi, 0, 0, mi, 0))]
    for dil in DILATIONS[1:]:
        qkv_shapes.append(jax.ShapeDtypeStruct((b, 3, HEADS, dil, s // dil, HEAD_DIM), BF16))
        qkv_specs.append(pl.BlockSpec((1, 3, HEADS, dil, tm // dil, HEAD_DIM),
                                      lambda bi, mi: (bi, 0, 0, 0, mi, 0)))
    return pl.pallas_call(
        functools.partial(_even_in_kernel, layer),
        grid=(b, s // tm),
        in_specs=[
            pl.BlockSpec((1, tm, d), lambda bi, mi: (bi, mi, 0)),
            _in_hbm(),
            pl.BlockSpec((4, tm, HEAD_DIM), lambda bi, mi: (0, mi, 0)),
        ],
        out_specs=qkv_specs + [pl.BlockSpec((1, tm, CONF_WIDTH), lambda bi, mi: (bi, mi, 0))],
        out_shape=qkv_shapes + [jax.ShapeDtypeStruct((b, s, CONF_WIDTH), F32)],
        scratch_shapes=[pltpu.VMEM((2, HEADS, tm, HEAD_DIM), F32), _weight_scratch(w_in)] + _staging_scratch(),
        compiler_params=pltpu.CompilerParams(
            dimension_semantics=("arbitrary", "arbitrary"), vmem_limit_bytes=V7X_VMEM_LIMIT),
        name="even_in_proj",
    )(x, w_in, rope_tab)


ATT_HEADS = 4
ATT_BATCH = 8


def _attn_head(head, g0_ref, g1_ref, g2_ref, o_ref, kk_ref, vv_ref, out_ref, lse_ref, stage_ref):
    seq = o_ref.shape[1]
    nb = ATT_BLOCK
    n_blocks = seq // nb
    qi = lax.broadcasted_iota(jnp.int32, (nb, 1), 0)
    kj = lax.broadcasted_iota(jnp.int32, (1, 2 * nb), 1)
    cur_only = (kj >= nb) & (kj <= qi + nb)
    band = (kj >= qi) & (kj <= qi + nb)

    for g, ref, dil in zip(range(3), (g0_ref, g1_ref, g2_ref), DILATIONS):
        per_res = n_blocks // dil
        firsts = range(0, n_blocks, per_res)
        if per_res > 1:
            for j, win_ref in ((1, kk_ref), (2, vv_ref)):
                win_ref[:, nb:, :] = ref[0, j, head]
                win_ref[1:, 0:nb, :] = ref[0, j, head, 0:n_blocks - 1]
                for i in firsts:
                    win_ref[i, 0:nb, :] = jnp.zeros((nb, HEAD_DIM), BF16)
            keys, vals, lo = (lambda i: kk_ref[i]), (lambda i: vv_ref[i]), 0
        else:
            keys, vals, lo = (lambda i: ref[0, 1, head, i]), (lambda i: ref[0, 2, head, i]), nb

        for c0 in range(0, n_blocks, ATT_BATCH):
            blocks = list(range(c0, c0 + ATT_BATCH))
            s = [lax.dot_general(ref[0, 0, head, i], keys(i), (((1,), (1,)), ((), ())),
                                 preferred_element_type=F32) for i in blocks]
            masks = [(cur_only if i in firsts else band)[:, lo:] for i in blocks]
            s = [jnp.where(mk, v, NEG_INF) for mk, v in zip(masks, s)]
            m = [jnp.max(v, axis=-1, keepdims=True) for v in s]
            ex = [jnp.exp2(v - mx) for v, mx in zip(s, m)]
            dens = [jnp.sum(e, axis=-1, keepdims=True) for e in ex]
            acc = [_dot(e.astype(BF16), vals(i)) for e, i in zip(ex, blocks)]
            for i, a, mx, den in zip(blocks, acc, m, dens):
                out = a[:, 0:nb] / den
                lse = jnp.broadcast_to(mx + jnp.log2(den), (nb, HEAD_DIM))
                start = (i % per_res) * nb * dil + i // per_res
                if dil == SPLIT_STRIDE * SPLIT_STRIDE:
                    part, row0 = start % SPLIT_STRIDE, start // SPLIT_STRIDE
                    rows = pl.ds(row0, nb, stride=SPLIT_STRIDE)
                    stage_ref[0, part, rows, :] = out
                    stage_ref[1, part, rows, :] = lse
                else:
                    rows = pl.ds(start, nb) if dil == 1 else pl.ds(start, nb, stride=dil)
                    out_ref[g, rows, :] = out
                    lse_ref[g, rows, :] = lse
        if dil == SPLIT_STRIDE * SPLIT_STRIDE:
            for part in range(SPLIT_STRIDE):
                rows = pl.ds(part, seq // SPLIT_STRIDE, stride=SPLIT_STRIDE)
                out_ref[g, rows, :] = stage_ref[0, part]
                lse_ref[g, rows, :] = stage_ref[1, part]

    chunk = 256
    for c in range(seq // chunk):
        rows = pl.ds(c * chunk, chunk)
        lse = [lse_ref[g, rows, :] for g in range(3)]
        top = jnp.maximum(jnp.maximum(lse[0], lse[1]), lse[2])
        num = jnp.zeros((chunk, HEAD_DIM), F32)
        den = jnp.zeros((chunk, HEAD_DIM), F32)
        for g in range(3):
            a = jnp.exp2(lse[g] - top)
            num = num + a * out_ref[g, rows, :]
            den = den + a
        o_ref[0, rows, head * HEAD_DIM:(head + 1) * HEAD_DIM] = (num / den).astype(o_ref.dtype)


def _attn_kernel(g0_ref, g1_ref, g2_ref, o_ref, *scratch):
    for head in range(g0_ref.shape[2]):
        _attn_head(head, g0_ref, g1_ref, g2_ref, o_ref, *scratch)


def _attention(g0, g1, g2):
    b, _, _, s, e = g0.shape
    nb = ATT_BLOCK
    blocked = [a.reshape(b, 3, HEADS, s // nb, nb, e) for a in (g0, g1, g2)]
    spec = pl.BlockSpec((1, 3, ATT_HEADS, s // nb, nb, e), lambda bi, hi: (bi, 0, hi, 0, 0, 0))
    return pl.pallas_call(
        _attn_kernel,
        grid=(b, HEADS // ATT_HEADS),
        in_specs=[spec] * 3,
        out_specs=pl.BlockSpec((1, s, ATT_HEADS * e), lambda bi, hi: (bi, 0, hi)),
        out_shape=jax.ShapeDtypeStruct((b, s, GROUP_WIDTH), BF16),
        scratch_shapes=[pltpu.VMEM((s // nb, 2 * nb, e), BF16)] * 2 + [pltpu.VMEM((3, s, e), F32)] * 2
        + [pltpu.VMEM((2, SPLIT_STRIDE, s // SPLIT_STRIDE, e), F32)],
        compiler_params=pltpu.CompilerParams(
            dimension_semantics=("arbitrary", "arbitrary"), vmem_limit_bytes=V7X_VMEM_LIMIT),
        name="dilated_attention",
    )(*blocked)


def _swiglu_ple(x1, p_rows, hid_ref, w_in_ref, w_out_ref, g_ref, b_ref, wp_ref, wg_ref, side_work=None):
    xb = x1.astype(BF16)
    for c in range(FFN_HIDDEN // FFN_CHUNK):
        cols = slice(c * FFN_CHUNK, (c + 1) * FFN_CHUNK)
        gate = _dot(xb, w_in_ref[:, cols])
        up = _dot(xb, w_in_ref[:, FFN_HIDDEN + c * FFN_CHUNK:FFN_HIDDEN + (c + 1) * FFN_CHUNK])
        hid_ref[:, cols] = (gate * _sigmoid(gate) * up).astype(BF16)
        if side_work is not None:
            side_work(c, gate[gate.shape[0] - SUBLANES:, 0:LANES])
    y = _dot(hid_ref[...], w_out_ref[...])
    x2 = _layer_norm(DN_ALPHA * x1 + y, g_ref[...], b_ref[...])
    emb = _dot(p_rows.astype(BF16), wp_ref[...])
    return x2 + emb * _sigmoid(_dot(x2.astype(BF16), wg_ref[...]))


def _ffn_specs(ffn_params, layer):
    w_in, w_out, ln_g, ln_b, w_proj, w_gate = ffn_params
    return [_in_hbm(), _in_hbm(), _layer_slab(ln_g, layer), _layer_slab(ln_b, layer), _in_hbm(), _in_hbm()]


def _ffn_scratch(ffn_params):
    w_in, w_out, _, _, w_proj, w_gate = ffn_params
    return [_weight_scratch(a) for a in (w_in, w_out, w_proj, w_gate)]


def _ffn_jobs(layer, ffn_in, ffn_bufs):
    w_in_hbm, w_out_hbm, g_ref, b_ref, wp_hbm, wg_hbm = ffn_in
    w_in_ref, w_out_ref, wp_ref, wg_ref = ffn_bufs
    jobs = [(w_in_hbm.at[layer], w_in_ref), (w_out_hbm.at[layer], w_out_ref),
            (wp_hbm.at[layer], wp_ref), (wg_hbm.at[layer], wg_ref)]
    return jobs, (w_in_ref, w_out_ref, g_ref, b_ref, wp_ref, wg_ref)


def _zero_from(anchor):
    bits = pltpu.bitcast(anchor, jnp.int32)
    return lax.shift_right_logical(lax.shift_right_logical(bits, 16), 16).astype(F32)[0:1, :]


def _conv_block(hist_ref, pre_ref, cw_ref, cb_ref, row_block, lane_block, after):
    first = CONV_PAD - (CONF_TAPS - 1)
    lanes = slice(lane_block * LANES, (lane_block + 1) * LANES)
    t0 = row_block * CONV_ROWS
    win = hist_ref[t0:t0 + CONV_ROWS + CONV_PAD + CONV_TAIL, lanes]
    for value in after:
        win = win + _zero_from(value)
    acc = jnp.broadcast_to(cb_ref[:, lanes], (CONV_ROWS, LANES))
    for s in range(SUBLANES):
        base, shift = divmod(first + s, SUBLANES)
        part = None
        for j in range(s, CONF_TAPS, SUBLANES):
            k0 = (j - s) + base * SUBLANES
            term = cw_ref[j:j + 1, lanes] * win[k0:k0 + CONV_ROWS + SUBLANES]
            part = term if part is None else part + term
        acc = acc + _shift_rows_up(part, shift)
    pre_ref[t0:t0 + CONV_ROWS, lanes] = acc
    return acc[0:SUBLANES]


def _conv_blocks(hist_ref, pre_ref, dst_ref, conv_refs, blocks, after):
    cw_ref, cb_ref, lg_ref, lb_ref = conv_refs
    for rb, lb in blocks:
        after = [_conv_block(hist_ref, pre_ref, cw_ref, cb_ref, rb, lb, after)]
        if lb == CONF_WIDTH // LANES - 1:
            rows = slice(rb * CONV_ROWS, (rb + 1) * CONV_ROWS)
            y = _layer_norm(pre_ref[rows, :], lg_ref[...], lb_ref[...])
            dst_ref[rows, :] = (y * _sigmoid(y)).astype(BF16)
    return after


def _even_tail_kernel(tiles_per_seq, mixer_layer, layer, attn_ref, glu0_ref, glu_next_ref, x_ref, p_ref, w_hbm,
                      mg_ref, mb_ref, cw_ref, cb_ref, lg_ref, lb_ref, *rest):
    ffn_in, rest = rest[:6], rest[6:]
    o_ref, hid_ref, hist_ref, pre_ref, conv_cur_ref, conv_next_ref, w_ref, *ffn_bufs, stage_ref, sem_ref = rest
    tm = x_ref.shape[0]
    step = pl.program_id(0)
    ffn_jobs, ffn_refs = _ffn_jobs(layer, ffn_in, ffn_bufs)

    @pl.when(step == 0)
    def _():
        _fetch_weights_bf16([(w_hbm.at[mixer_layer], w_ref)] + ffn_jobs, stage_ref, sem_ref)

    conv_refs = (cw_ref, cb_ref, lg_ref, lb_ref)
    blocks = [(rb, lb) for rb in range(tm // CONV_ROWS) for lb in range(CONF_WIDTH // LANES)]

    def stage(glu_ref, conv_tile):
        hist_ref[CONV_PAD:CONV_PAD + tm, :] = glu_ref[...]
        return (conv_tile + 1) % tiles_per_seq != 0

    def carry(keep):
        tail = hist_ref[tm:tm + CONV_PAD, :]
        hist_ref[0:CONV_PAD, :] = jnp.where(keep, tail, 0.0)

    @pl.when(step == 0)
    def _():
        hist_ref[...] = jnp.zeros(hist_ref.shape, F32)
        keep = stage(glu0_ref, step)
        zero = jnp.zeros((SUBLANES, LANES), F32)
        _conv_blocks(hist_ref, pre_ref, conv_cur_ref, conv_refs, blocks, [zero])
        carry(keep)

    mix = _dot(attn_ref[...], w_ref[0:GROUP_WIDTH, :]) + _dot(conv_cur_ref[...], w_ref[GROUP_WIDTH:, :])
    x1 = _layer_norm(DN_ALPHA * x_ref[...] + mix, mg_ref[...], mb_ref[...])

    keep = stage(glu_next_ref, step + 1)
    chunks = FFN_HIDDEN // FFN_CHUNK
    per_chunk = -(-len(blocks) // chunks)
    state = {"after": []}

    def side_work(c, anchor):
        todo = blocks[c * per_chunk:(c + 1) * per_chunk]
        state["after"] = _conv_blocks(hist_ref, pre_ref, conv_next_ref, conv_refs, todo, [anchor] + state["after"])

    o_ref[...] = _swiglu_ple(x1, p_ref[...], hid_ref, *ffn_refs, side_work=side_work)
    carry(keep)
    conv_cur_ref[...] = conv_next_ref[...]


def _even_tail(attn, glu, x2d, p_all, w_out, conv_params, mixer_layer, ln_g, ln_b, ffn_params, layer, tiles_per_seq):
    n, d = x2d.shape
    tm = ROW_TILE
    last = n // tm - 1
    return pl.pallas_call(
        functools.partial(_even_tail_kernel, tiles_per_seq, mixer_layer, layer),
        grid=(n // tm,),
        in_specs=[
            pl.BlockSpec((tm, GROUP_WIDTH), lambda i: (i, 0)),
            pl.BlockSpec((tm, CONF_WIDTH), lambda i: (0, 0)),
            pl.BlockSpec((tm, CONF_WIDTH), lambda i: (jnp.minimum(i + 1, last), 0)),
            pl.BlockSpec((tm, d), lambda i: (i, 0)),
            pl.BlockSpec((None, tm, PLE_DIM), lambda i: (layer, i, 0)),
            _in_hbm(), _layer_slab(ln_g, layer), _layer_slab(ln_b, layer),
        ] + [_layer_slab(a, mixer_layer) for a in conv_params] + _ffn_specs(ffn_params, layer),
        out_specs=pl.BlockSpec((tm, d), lambda i: (i, 0)),
        out_shape=jax.ShapeDtypeStruct((n, d), F32),
        scratch_shapes=[pltpu.VMEM((tm, FFN_HIDDEN), BF16),
                        pltpu.VMEM((CONV_PAD + tm + CONV_TAIL, CONF_WIDTH), F32),
                        pltpu.VMEM((tm, CONF_WIDTH), F32),
                        pltpu.VMEM((tm, CONF_WIDTH), BF16), pltpu.VMEM((tm, CONF_WIDTH), BF16),
                        _weight_scratch(w_out)] + _ffn_scratch(ffn_params) + _staging_scratch(),
        compiler_params=pltpu.CompilerParams(
            dimension_semantics=("arbitrary",), vmem_limit_bytes=V7X_VMEM_LIMIT),
        name="conformer_out_proj_swiglu_ple",
    )(attn, glu, glu, x2d, p_all, w_out, ln_g, ln_b, *conv_params, *ffn_params)


SHORT_CARRY = 8


def _odd_layer_kernel(mixer_layer, layer, x_ref, p_ref, w_in_hbm, cw_ref, w_out_hbm, mg_ref, mb_ref, *rest):
    ffn_in, rest = rest[:6], rest[6:]
    o_ref, gate_ref, mix_ref, hid_ref, w_in_ref, w_out_ref, *ffn_bufs, stage_ref, sem_ref = rest
    ffn_jobs, ffn_refs = _ffn_jobs(layer, ffn_in, ffn_bufs)

    @pl.when((pl.program_id(0) == 0) & (pl.program_id(1) == 0))
    def _():
        mixer_jobs = [(w_in_hbm.at[mixer_layer], w_in_ref), (w_out_hbm.at[mixer_layer], w_out_ref)]
        _fetch_weights_bf16(mixer_jobs + ffn_jobs, stage_ref, sem_ref)

    tm = x_ref.shape[1]
    width = D_MODEL
    chunk = 512

    @pl.when(pl.program_id(1) == 0)
    def _():
        gate_ref[0:SHORT_CARRY, :] = jnp.zeros((SHORT_CARRY, width), F32)

    xb = x_ref[0].astype(BF16)
    for c in range(width // chunk):
        cols = slice(c * chunk, (c + 1) * chunk)
        cg = _dot(xb, w_in_ref[:, width + c * chunk:width + (c + 1) * chunk])
        hh = _dot(xb, w_in_ref[:, 2 * width + c * chunk:2 * width + (c + 1) * chunk])
        gate_ref[SHORT_CARRY:, cols] = cg * hh
        y = jnp.zeros((tm, chunk), F32)
        for j in range(SHORT_TAPS):
            off = SHORT_CARRY - (SHORT_TAPS - 1) + j
            y = y + cw_ref[j:j + 1, cols] * gate_ref[off:off + tm, cols]
        bg = _dot(xb, w_in_ref[:, cols])
        mix_ref[:, cols] = (bg * y).astype(BF16)
        gate_ref[0:SHORT_CARRY, cols] = gate_ref[tm:tm + SHORT_CARRY, cols]
    mix = _dot(mix_ref[...], w_out_ref[...])
    x1 = _layer_norm(DN_ALPHA * x_ref[0] + mix, mg_ref[...], mb_ref[...])
    o_ref[0] = _swiglu_ple(x1, p_ref[...], hid_ref, *ffn_refs)


def _odd_layer(x, p_all, w_in, conv_w, w_out, mixer_layer, ln_g, ln_b, ffn_params, layer):
    b, s, d = x.shape
    tm = ROW_TILE
    tiles = s // tm
    return pl.pallas_call(
        functools.partial(_odd_layer_kernel, mixer_layer, layer),
        grid=(b, tiles),
        in_specs=[
            pl.BlockSpec((1, tm, d), lambda bi, mi: (bi, mi, 0)),
            pl.BlockSpec((None, tm, PLE_DIM), lambda bi, mi: (layer, bi * tiles + mi, 0)),
            _in_hbm(), _layer_slab(conv_w, mixer_layer), _in_hbm(),
            _layer_slab(ln_g, layer), _layer_slab(ln_b, layer),
        ] + _ffn_specs(ffn_params, layer),
        out_specs=pl.BlockSpec((1, tm, d), lambda bi, mi: (bi, mi, 0)),
        out_shape=jax.ShapeDtypeStruct((b, s, d), F32),
        scratch_shapes=[pltpu.VMEM((SHORT_CARRY + tm, d), F32), pltpu.VMEM((tm, d), BF16),
                        pltpu.VMEM((tm, FFN_HIDDEN), BF16), _weight_scratch(w_in), _weight_scratch(w_out)]
        + _ffn_scratch(ffn_params) + _staging_scratch(),
        compiler_params=pltpu.CompilerParams(
            dimension_semantics=("arbitrary", "arbitrary"), vmem_limit_bytes=V7X_VMEM_LIMIT),
        name="short_conv_mixer_swiglu_ple",
    )(x, p_all, w_in, conv_w, w_out, ln_g, ln_b, *ffn_params)


def _rope_tables(seq):
    half = HEAD_DIM // 2
    inv = ROPE_THETA ** (-np.arange(half, dtype=np.float64) / half)
    ang = np.arange(seq, dtype=np.float64)[:, None] * inv[None, :]
    cos = np.concatenate([np.cos(ang), np.cos(ang)], axis=-1)
    sin = np.concatenate([-np.sin(ang), np.sin(ang)], axis=-1)
    scale = HEAD_DIM ** -0.5 * np.log2(np.e)
    return np.stack([cos * scale, sin * scale, cos, sin]).astype(np.float32)


def kernel(x, p, even_w_in, even_w_out, conf_conv_w, conf_conv_b, conf_ln_g, conf_ln_b, odd_w_in, odd_conv_w, odd_w_out, ln_mix_g, ln_mix_b, ln_ffn_g, ln_ffn_b, ffn_w_in, ffn_w_out, ple_w_proj, ple_w_gate):
    b, s, d = x.shape
    depth = p.shape[0]
    rope_tab = _rope_tables(s)
    p_all = p.reshape(depth, b * s, PLE_DIM)
    conf_conv_b, conf_ln_g, conf_ln_b, ln_mix_g, ln_mix_b, ln_ffn_g, ln_ffn_b = (
        _rows(v) for v in (conf_conv_b, conf_ln_g, conf_ln_b, ln_mix_g, ln_mix_b, ln_ffn_g, ln_ffn_b))
    for i in range(depth):
        j = i // 2
        ffn_params = (ffn_w_in, ffn_w_out, ln_ffn_g, ln_ffn_b, ple_w_proj, ple_w_gate)
        if i % 2 == 0:
            g0, g1, g2, glu = _even_in(x, even_w_in, j, rope_tab)
            attn = _attention(g0, g1, g2)
            conv_params = (conf_conv_w, conf_conv_b, conf_ln_g, conf_ln_b)
            x = _even_tail(attn.reshape(b * s, -1), glu.reshape(b * s, -1), x.reshape(b * s, d), p_all,
                           even_w_out, conv_params, j, ln_mix_g, ln_mix_b, ffn_params, i,
                           s // ROW_TILE).reshape(b, s, d)
        else:
            x = _odd_layer(x, p_all, odd_w_in, odd_conv_w, odd_w_out, j, ln_mix_g, ln_mix_b, ffn_params, i)
    return x
```

```python
import functools

import jax
import jax.numpy as jnp
import numpy as np
from jax import lax
from jax.experimental import pallas as pl
from jax.experimental.pallas import tpu as pltpu

F32 = jnp.float32
BF16 = jnp.bfloat16

D_MODEL = 1024
HEADS = 4
HEAD_DIM = 128
DILATIONS = (1, 4, 16)
ATT_BLOCK = 128
GROUP_WIDTH = HEADS * HEAD_DIM
QK_WIDTH = len(DILATIONS) * GROUP_WIDTH
CONF_WIDTH = 512
CONF_TAPS = 31
SHORT_TAPS = 3
FFN_HIDDEN = 2816
PLE_DIM = 256
ROPE_THETA = 10000.0
NEG_INF = -1e30
LN_EPS = 1e-5
DN_ALPHA = float(4 ** 0.25)

SPLIT_STRIDE = 4
ROW_TILE = 512
FFN_CHUNK = 256
CONV_PAD = 32
CONV_ROWS = 64
SUBLANES = 8
LANES = 128
CONV_TAIL = SUBLANES
V7X_VMEM_LIMIT = 56 * 1024 * 1024


def _dot(a, b):
    return jnp.dot(a, b, preferred_element_type=F32)


def _layer_norm(v, g, b):
    mu = jnp.mean(v, axis=-1, keepdims=True)
    c = v - mu
    var = jnp.mean(c * c, axis=-1, keepdims=True)
    return c * lax.rsqrt(var + LN_EPS) * g + b


def _sigmoid(v):
    return 0.5 * jnp.tanh(0.5 * v) + 0.5


def _shift_rows_up(v, shift):
    rows = v.shape[0] - SUBLANES
    if shift == 0:
        return v[0:rows]
    tiles = [pltpu.roll(v[k:k + SUBLANES], SUBLANES - shift, 0) for k in range(0, rows + SUBLANES, SUBLANES)]
    low = lax.broadcasted_iota(jnp.int32, (SUBLANES, v.shape[1]), 0) < SUBLANES - shift
    return jnp.concatenate([jnp.where(low, a, b) for a, b in zip(tiles[:-1], tiles[1:])], axis=0)


def _resident(shape):
    return pl.BlockSpec(shape, lambda *_: (0,) * len(shape), pipeline_mode=pl.Buffered(1))


def _layer_slab(stacked, layer):
    return pl.BlockSpec((None,) + stacked.shape[1:], lambda *_: (layer, 0, 0), pipeline_mode=pl.Buffered(1))


def _rows(stacked):
    return stacked.reshape(stacked.shape[0], 1, stacked.shape[1])


STAGE_ROWS = 512
STAGE_COLS = 512
STAGE_SLOTS = 4


def _in_hbm():
    return pl.BlockSpec(memory_space=pl.ANY)


def _weight_scratch(stacked):
    return pltpu.VMEM(stacked.shape[1:], BF16)


def _staging_scratch():
    return [pltpu.VMEM((STAGE_SLOTS, STAGE_ROWS, STAGE_COLS), F32), pltpu.SemaphoreType.DMA((STAGE_SLOTS,))]


def _fetch_weights_bf16(jobs, stage_ref, sem_ref):
    ahead = STAGE_SLOTS - 1
    chunks = [(src, dst, r0, min(STAGE_ROWS, dst.shape[0] - r0), c0)
              for src, dst in jobs
              for c0 in range(0, dst.shape[1], STAGE_COLS)
              for r0 in range(0, dst.shape[0], STAGE_ROWS)]

    def copy(i):
        src, _, r0, rows, c0 = chunks[i]
        return pltpu.make_async_copy(src.at[pl.ds(r0, rows), pl.ds(c0, STAGE_COLS)],
                                     stage_ref.at[i % STAGE_SLOTS, pl.ds(0, rows), :], sem_ref.at[i % STAGE_SLOTS])

    for i in range(min(ahead, len(chunks))):
        copy(i).start()
    for i, (_, dst, r0, rows, c0) in enumerate(chunks):
        if i + ahead < len(chunks):
            copy(i + ahead).start()
        copy(i).wait()
        dst[r0:r0 + rows, c0:c0 + STAGE_COLS] = stage_ref[i % STAGE_SLOTS, 0:rows, :].astype(BF16)


def _even_in_kernel(layer, x_ref, w_hbm, rope_ref, o0_ref, o1_ref, o2_ref, glu_ref, slab_ref, w_ref,
                    stage_ref, sem_ref):
    @pl.when((pl.program_id(0) == 0) & (pl.program_id(1) == 0))
    def _():
        _fetch_weights_bf16([(w_hbm.at[layer], w_ref)], stage_ref, sem_ref)

    tm = x_ref.shape[1]
    xb = x_ref[0].astype(BF16)
    outs = (o0_ref, o1_ref, o2_ref)
    a = _dot(xb, w_ref[:, 3 * QK_WIDTH:3 * QK_WIDTH + CONF_WIDTH])
    gate = _dot(xb, w_ref[:, 3 * QK_WIDTH + CONF_WIDTH:])
    glu_ref[0] = a * _sigmoid(gate)
    for j in range(3):
        for g, dil in reversed(list(enumerate(DILATIONS))):
            c0 = j * QK_WIDTH + g * GROUP_WIDTH
            acc = _dot(xb, w_ref[:, c0:c0 + GROUP_WIDTH])
            slab, regroup = slab_ref.at[0], slab_ref.at[1]
            for h in range(HEADS):
                t = acc[:, h * HEAD_DIM:(h + 1) * HEAD_DIM]
                if j < 2:
                    t = t * rope_ref[2 * j] + pltpu.roll(t, HEAD_DIM // 2, 1) * rope_ref[2 * j + 1]
                if dil == 1:
                    o0_ref[0, j, h] = t.astype(BF16)
                else:
                    slab[h] = t
            if dil == SPLIT_STRIDE:
                for h in range(HEADS):
                    for r in range(dil):
                        outs[g][0, j, h, r] = slab[h, pl.ds(r, tm // dil, stride=dil), :].astype(BF16)
            elif dil == SPLIT_STRIDE * SPLIT_STRIDE:
                part = tm // SPLIT_STRIDE
                for h in range(HEADS):
                    for r in range(SPLIT_STRIDE):
                        regroup[h, r * part:(r + 1) * part, :] = slab[h, pl.ds(r, part, stride=SPLIT_STRIDE), :]
                for h in range(HEADS):
                    for r in range(dil):
                        lo, hi = r % SPLIT_STRIDE, r // SPLIT_STRIDE
                        rows = pl.ds(lo * part + hi, tm // dil, stride=SPLIT_STRIDE)
                        outs[g][0, j, h, r] = regroup[h, rows, :].astype(BF16)


def _even_in(x, w_in, layer, rope_tab):
    b, s, d = x.shape
    tm = ROW_TILE
    qkv_shapes = [jax.ShapeDtypeStruct((b, 3, HEADS, s, HEAD_DIM), BF16)]
    qkv_specs = [pl.BlockSpec((1, 3, HEADS, tm, HEAD_DIM), lambda bi, mi: (bi, 0, 0, mi, 0))]
    for dil in DILATIONS[1:]:
        qkv_shapes.append(jax.ShapeDtypeStruct((b, 3, HEADS, dil, s // dil, HEAD_DIM), BF16))
        qkv_specs.append(pl.BlockSpec((1, 3, HEADS, dil, tm // dil, HEAD_DIM),
                                      lambda bi, mi: (bi, 0, 0, 0, mi, 0)))
    return pl.pallas_call(
        functools.partial(_even_in_kernel, layer),
        grid=(b, s // tm),
        in_specs=[
            pl.BlockSpec((1, tm, d), lambda bi, mi: (bi, mi, 0)),
            _in_hbm(),
            pl.BlockSpec((4, tm, HEAD_DIM), lambda bi, mi: (0, mi, 0)),
        ],
        out_specs=qkv_specs + [pl.BlockSpec((1, tm, CONF_WIDTH), lambda bi, mi: (bi, mi, 0))],
        out_shape=qkv_shapes + [jax.ShapeDtypeStruct((b, s, CONF_WIDTH), F32)],
        scratch_shapes=[pltpu.VMEM((2, HEADS, tm, HEAD_DIM), F32), _weight_scratch(w_in)] + _staging_scratch(),
        compiler_params=pltpu.CompilerParams(
            dimension_semantics=("arbitrary", "arbitrary"), vmem_limit_bytes=V7X_VMEM_LIMIT),
        name="even_in_proj",
    )(x, w_in, rope_tab)


ATT_HEADS = 2
ATT_BATCH = 16


def _attn_head(head, g0_ref, g1_ref, g2_ref, o_ref, kk_ref, vv_ref, out_ref, lse_ref, stage_ref):
    seq = o_ref.shape[1]
    nb = ATT_BLOCK
    n_blocks = seq // nb
    qi = lax.broadcasted_iota(jnp.int32, (nb, 1), 0)
    kj = lax.broadcasted_iota(jnp.int32, (1, 2 * nb), 1)
    cur_only = (kj >= nb) & (kj <= qi + nb)
    band = (kj >= qi) & (kj <= qi + nb)

    for g, ref, dil in reversed(list(zip(range(3), (g0_ref, g1_ref, g2_ref), DILATIONS))):
        per_res = n_blocks // dil
        firsts = range(0, n_blocks, per_res)
        if per_res > 1:
            for j, win_ref in ((1, kk_ref), (2, vv_ref)):
                win_ref[:, nb:, :] = ref[0, j, head]
                win_ref[1:, 0:nb, :] = ref[0, j, head, 0:n_blocks - 1]
                for i in firsts:
                    win_ref[i, 0:nb, :] = jnp.zeros((nb, HEAD_DIM), BF16)
            keys, vals, lo = (lambda i: kk_ref[i]), (lambda i: vv_ref[i]), 0
        else:
            keys, vals, lo = (lambda i: ref[0, 1, head, i]), (lambda i: ref[0, 2, head, i]), nb

        for c0 in range(0, n_blocks, ATT_BATCH):
            blocks = list(range(c0, c0 + ATT_BATCH))
            s = [lax.dot_general(ref[0, 0, head, i], keys(i), (((1,), (1,)), ((), ())),
                                 preferred_element_type=F32) for i in blocks]
            masks = [(cur_only if i in firsts else band)[:, lo:] for i in blocks]
            s = [jnp.where(mk, v, NEG_INF) for mk, v in zip(masks, s)]
            m = [jnp.max(v, axis=-1, keepdims=True) for v in s]
            ex = [jnp.exp2(v - mx) for v, mx in zip(s, m)]
            dens = [jnp.sum(e, axis=-1, keepdims=True) for e in ex]
            acc = [_dot(e.astype(BF16), vals(i)) for e, i in zip(ex, blocks)]
            for i, a, mx, den in zip(blocks, acc, m, dens):
                out = a[:, 0:nb] / den
                lse = jnp.broadcast_to(mx + jnp.log2(den), (nb, HEAD_DIM))
                start = (i % per_res) * nb * dil + i // per_res
                if dil == SPLIT_STRIDE * SPLIT_STRIDE:
                    part, row0 = start % SPLIT_STRIDE, start // SPLIT_STRIDE
                    rows = pl.ds(row0, nb, stride=SPLIT_STRIDE)
                    stage_ref[0, part, rows, :] = out
                    stage_ref[1, part, rows, :] = lse
                else:
                    rows = pl.ds(start, nb) if dil == 1 else pl.ds(start, nb, stride=dil)
                    out_ref[g, rows, :] = out
                    lse_ref[g, rows, :] = lse
        if dil == SPLIT_STRIDE * SPLIT_STRIDE:
            for part in range(SPLIT_STRIDE):
                rows = pl.ds(part, seq // SPLIT_STRIDE, stride=SPLIT_STRIDE)
                out_ref[g, rows, :] = stage_ref[0, part]
                lse_ref[g, rows, :] = stage_ref[1, part]

    chunk = 256
    for c in range(seq // chunk):
        rows = pl.ds(c * chunk, chunk)
        lse = [lse_ref[g, rows, :] for g in range(3)]
        top = jnp.maximum(jnp.maximum(lse[0], lse[1]), lse[2])
        num = jnp.zeros((chunk, HEAD_DIM), F32)
        den = jnp.zeros((chunk, HEAD_DIM), F32)
        for g in range(3):
            a = jnp.exp2(lse[g] - top)
            num = num + a * out_ref[g, rows, :]
            den = den + a
        o_ref[0, rows, head * HEAD_DIM:(head + 1) * HEAD_DIM] = (num / den).astype(o_ref.dtype)


def _attn_kernel(g0_ref, g1_ref, g2_ref, o_ref, *scratch):
    for head in range(g0_ref.shape[2]):
        _attn_head(head, g0_ref, g1_ref, g2_ref, o_ref, *scratch)


def _attention(g0, g1, g2):
    b, _, _, s, e = g0.shape
    nb = ATT_BLOCK
    blocked = [a.reshape(b, 3, HEADS, s // nb, nb, e) for a in (g0, g1, g2)]
    spec = pl.BlockSpec((1, 3, ATT_HEADS, s // nb, nb, e), lambda bi, hi: (bi, 0, hi, 0, 0, 0))
    return pl.pallas_call(
        _attn_kernel,
        grid=(b, HEADS // ATT_HEADS),
        in_specs=[spec] * 3,
        out_specs=pl.BlockSpec((1, s, ATT_HEADS * e), lambda bi, hi: (bi, 0, hi)),
        out_shape=jax.ShapeDtypeStruct((b, s, GROUP_WIDTH), BF16),
        scratch_shapes=[pltpu.VMEM((s // nb, 2 * nb, e), BF16)] * 2 + [pltpu.VMEM((3, s, e), F32)] * 2
        + [pltpu.VMEM((2, SPLIT_STRIDE, s // SPLIT_STRIDE, e), F32)],
        compiler_params=pltpu.CompilerParams(
            dimension_semantics=("arbitrary", "arbitrary"), vmem_limit_bytes=V7X_VMEM_LIMIT),
        name="dilated_attention",
    )(*blocked)


def _swiglu_ple(x1, p_rows, hid_ref, w_in_ref, w_out_ref, g_ref, b_ref, wp_ref, wg_ref, side_work=None):
    xb = x1.astype(BF16)
    for c in range(FFN_HIDDEN // FFN_CHUNK):
        cols = slice(c * FFN_CHUNK, (c + 1) * FFN_CHUNK)
        gate = _dot(xb, w_in_ref[:, cols])
        up = _dot(xb, w_in_ref[:, FFN_HIDDEN + c * FFN_CHUNK:FFN_HIDDEN + (c + 1) * FFN_CHUNK])
        hid_ref[:, cols] = (gate * _sigmoid(gate) * up).astype(BF16)
        if side_work is not None:
            side_work(c, gate[gate.shape[0] - SUBLANES:, 0:LANES])
    y = _dot(hid_ref[...], w_out_ref[...])
    x2 = _layer_norm(DN_ALPHA * x1 + y, g_ref[...], b_ref[...])
    emb = _dot(p_rows.astype(BF16), wp_ref[...])
    return x2 + emb * _sigmoid(_dot(x2.astype(BF16), wg_ref[...]))


def _ffn_specs(ffn_params, layer):
    w_in, w_out, ln_g, ln_b, w_proj, w_gate = ffn_params
    return [_in_hbm(), _in_hbm(), _layer_slab(ln_g, layer), _layer_slab(ln_b, layer), _in_hbm(), _in_hbm()]


def _ffn_scratch(ffn_params):
    w_in, w_out, _, _, w_proj, w_gate = ffn_params
    return [_weight_scratch(a) for a in (w_in, w_out, w_proj, w_gate)]


def _ffn_jobs(layer, ffn_in, ffn_bufs):
    w_in_hbm, w_out_hbm, g_ref, b_ref, wp_hbm, wg_hbm = ffn_in
    w_in_ref, w_out_ref, wp_ref, wg_ref = ffn_bufs
    jobs = [(w_in_hbm.at[layer], w_in_ref), (w_out_hbm.at[layer], w_out_ref),
            (wp_hbm.at[layer], wp_ref), (wg_hbm.at[layer], wg_ref)]
    return jobs, (w_in_ref, w_out_ref, g_ref, b_ref, wp_ref, wg_ref)


def _zero_from(anchor):
    bits = pltpu.bitcast(anchor, jnp.int32)
    return lax.shift_right_logical(lax.shift_right_logical(bits, 16), 16).astype(F32)[0:1, :]


def _conv_block(hist_ref, pre_ref, cw_ref, cb_ref, row_block, lane_block, after):
    first = CONV_PAD - (CONF_TAPS - 1)
    lanes = slice(lane_block * LANES, (lane_block + 1) * LANES)
    t0 = row_block * CONV_ROWS
    win = hist_ref[t0:t0 + CONV_ROWS + CONV_PAD + CONV_TAIL, lanes]
    for value in after:
        win = win + _zero_from(value)
    acc = jnp.broadcast_to(cb_ref[:, lanes], (CONV_ROWS, LANES))
    for s in range(SUBLANES):
        base, shift = divmod(first + s, SUBLANES)
        part = None
        for j in range(s, CONF_TAPS, SUBLANES):
            k0 = (j - s) + base * SUBLANES
            term = cw_ref[j:j + 1, lanes] * win[k0:k0 + CONV_ROWS + SUBLANES]
            part = term if part is None else part + term
        acc = acc + _shift_rows_up(part, shift)
    pre_ref[t0:t0 + CONV_ROWS, lanes] = acc
    return acc[0:SUBLANES]


def _conv_blocks(hist_ref, pre_ref, dst_ref, conv_refs, blocks, after):
    cw_ref, cb_ref, lg_ref, lb_ref = conv_refs
    for rb, lb in blocks:
        after = [_conv_block(hist_ref, pre_ref, cw_ref, cb_ref, rb, lb, after)]
        if lb == CONF_WIDTH // LANES - 1:
            rows = slice(rb * CONV_ROWS, (rb + 1) * CONV_ROWS)
            y = _layer_norm(pre_ref[rows, :], lg_ref[...], lb_ref[...])
            dst_ref[rows, :] = (y * _sigmoid(y)).astype(BF16)
    return after


def _even_tail_kernel(tiles_per_seq, mixer_layer, layer, attn_ref, glu0_ref, glu_next_ref, x_ref, p_ref, w_hbm,
                      mg_ref, mb_ref, cw_ref, cb_ref, lg_ref, lb_ref, *rest):
    ffn_in, rest = rest[:6], rest[6:]
    o_ref, hid_ref, hist_ref, pre_ref, conv_cur_ref, conv_next_ref, w_ref, *ffn_bufs, stage_ref, sem_ref = rest
    tm = x_ref.shape[0]
    step = pl.program_id(0)
    ffn_jobs, ffn_refs = _ffn_jobs(layer, ffn_in, ffn_bufs)

    @pl.when(step == 0)
    def _():
        _fetch_weights_bf16([(w_hbm.at[mixer_layer], w_ref)] + ffn_jobs, stage_ref, sem_ref)

    conv_refs = (cw_ref, cb_ref, lg_ref, lb_ref)
    blocks = [(rb, lb) for rb in range(tm // CONV_ROWS) for lb in range(CONF_WIDTH // LANES)]

    def stage(glu_ref, conv_tile):
        hist_ref[CONV_PAD:CONV_PAD + tm, :] = glu_ref[...]
        return (conv_tile + 1) % tiles_per_seq != 0

    def carry(keep):
        tail = hist_ref[tm:tm + CONV_PAD, :]
        hist_ref[0:CONV_PAD, :] = jnp.where(keep, tail, 0.0)

    @pl.when(step == 0)
    def _():
        hist_ref[...] = jnp.zeros(hist_ref.shape, F32)
        keep = stage(glu0_ref, step)
        zero = jnp.zeros((SUBLANES, LANES), F32)
        _conv_blocks(hist_ref, pre_ref, conv_cur_ref, conv_refs, blocks, [zero])
        carry(keep)

    mix = _dot(attn_ref[...], w_ref[0:GROUP_WIDTH, :]) + _dot(conv_cur_ref[...], w_ref[GROUP_WIDTH:, :])
    x1 = _layer_norm(DN_ALPHA * x_ref[...] + mix, mg_ref[...], mb_ref[...])

    keep = stage(glu_next_ref, step + 1)
    chunks = FFN_HIDDEN // FFN_CHUNK
    per_chunk = -(-len(blocks) // chunks)
    state = {"after": []}

    def side_work(c, anchor):
        todo = blocks[c * per_chunk:(c + 1) * per_chunk]
        state["after"] = _conv_blocks(hist_ref, pre_ref, conv_next_ref, conv_refs, todo, [anchor] + state["after"])

    o_ref[...] = _swiglu_ple(x1, p_ref[...], hid_ref, *ffn_refs, side_work=side_work)
    carry(keep)
    conv_cur_ref[...] = conv_next_ref[...]


def _even_tail(attn, glu, x2d, p_all, w_out, conv_params, mixer_layer, ln_g, ln_b, ffn_params, layer, tiles_per_seq):
    n, d = x2d.shape
    tm = ROW_TILE
    last = n // tm - 1
    return pl.pallas_call(
        functools.partial(_even_tail_kernel, tiles_per_seq, mixer_layer, layer),
        grid=(n // tm,),
        in_specs=[
            pl.BlockSpec((tm, GROUP_WIDTH), lambda i: (i, 0)),
            pl.BlockSpec((tm, CONF_WIDTH), lambda i: (0, 0)),
            pl.BlockSpec((tm, CONF_WIDTH), lambda i: (jnp.minimum(i + 1, last), 0)),
            pl.BlockSpec((tm, d), lambda i: (i, 0)),
            pl.BlockSpec((None, tm, PLE_DIM), lambda i: (layer, i, 0)),
            _in_hbm(), _layer_slab(ln_g, layer), _layer_slab(ln_b, layer),
        ] + [_layer_slab(a, mixer_layer) for a in conv_params] + _ffn_specs(ffn_params, layer),
        out_specs=pl.BlockSpec((tm, d), lambda i: (i, 0)),
        out_shape=jax.ShapeDtypeStruct((n, d), F32),
        scratch_shapes=[pltpu.VMEM((tm, FFN_HIDDEN), BF16),
                        pltpu.VMEM((CONV_PAD + tm + CONV_TAIL, CONF_WIDTH), F32),
                        pltpu.VMEM((tm, CONF_WIDTH), F32),
                        pltpu.VMEM((tm, CONF_WIDTH), BF16), pltpu.VMEM((tm, CONF_WIDTH), BF16),
                        _weight_scratch(w_out)] + _ffn_scratch(ffn_params) + _staging_scratch(),
        compiler_params=pltpu.CompilerParams(
            dimension_semantics=("arbitrary",), vmem_limit_bytes=V7X_VMEM_LIMIT),
        name="conformer_out_proj_swiglu_ple",
    )(attn, glu, glu, x2d, p_all, w_out, ln_g, ln_b, *conv_params, *ffn_params)


SHORT_CARRY = 8


def _odd_layer_kernel(mixer_layer, layer, x_ref, p_ref, w_in_hbm, cw_ref, w_out_hbm, mg_ref, mb_ref, *rest):
    ffn_in, rest = rest[:6], rest[6:]
    o_ref, gate_ref, mix_ref, hid_ref, w_in_ref, w_out_ref, *ffn_bufs, stage_ref, sem_ref = rest
    ffn_jobs, ffn_refs = _ffn_jobs(layer, ffn_in, ffn_bufs)

    @pl.when((pl.program_id(0) == 0) & (pl.program_id(1) == 0))
    def _():
        mixer_jobs = [(w_in_hbm.at[mixer_layer], w_in_ref), (w_out_hbm.at[mixer_layer], w_out_ref)]
        _fetch_weights_bf16(mixer_jobs + ffn_jobs, stage_ref, sem_ref)

    tm = x_ref.shape[1]
    width = D_MODEL
    chunk = 512

    @pl.when(pl.program_id(1) == 0)
    def _():
        gate_ref[0:SHORT_CARRY, :] = jnp.zeros((SHORT_CARRY, width), F32)

    xb = x_ref[0].astype(BF16)
    for c in range(width // chunk):
        cols = slice(c * chunk, (c + 1) * chunk)
        cg = _dot(xb, w_in_ref[:, width + c * chunk:width + (c + 1) * chunk])
        hh = _dot(xb, w_in_ref[:, 2 * width + c * chunk:2 * width + (c + 1) * chunk])
        gate_ref[SHORT_CARRY:, cols] = cg * hh
        y = jnp.zeros((tm, chunk), F32)
        for j in range(SHORT_TAPS):
            off = SHORT_CARRY - (SHORT_TAPS - 1) + j
            y = y + cw_ref[j:j + 1, cols] * gate_ref[off:off + tm, cols]
        bg = _dot(xb, w_in_ref[:, cols])
        mix_ref[:, cols] = (bg * y).astype(BF16)
        gate_ref[0:SHORT_CARRY, cols] = gate_ref[tm:tm + SHORT_CARRY, cols]
    mix = _dot(mix_ref[...], w_out_ref[...])
    x1 = _layer_norm(DN_ALPHA * x_ref[0] + mix, mg_ref[...], mb_ref[...])
    o_ref[0] = _swiglu_ple(x1, p_ref[...], hid_ref, *ffn_refs)


def _odd_layer(x, p_all, w_in, conv_w, w_out, mixer_layer, ln_g, ln_b, ffn_params, layer):
    b, s, d = x.shape
    tm = ROW_TILE
    tiles = s // tm
    return pl.pallas_call(
        functools.partial(_odd_layer_kernel, mixer_layer, layer),
        grid=(b, tiles),
        in_specs=[
            pl.BlockSpec((1, tm, d), lambda bi, mi: (bi, mi, 0)),
            pl.BlockSpec((None, tm, PLE_DIM), lambda bi, mi: (layer, bi * tiles + mi, 0)),
            _in_hbm(), _layer_slab(conv_w, mixer_layer), _in_hbm(),
            _layer_slab(ln_g, layer), _layer_slab(ln_b, layer),
        ] + _ffn_specs(ffn_params, layer),
        out_specs=pl.BlockSpec((1, tm, d), lambda bi, mi: (bi, mi, 0)),
        out_shape=jax.ShapeDtypeStruct((b, s, d), F32),
        scratch_shapes=[pltpu.VMEM((SHORT_CARRY + tm, d), F32), pltpu.VMEM((tm, d), BF16),
                        pltpu.VMEM((tm, FFN_HIDDEN), BF16), _weight_scratch(w_in), _weight_scratch(w_out)]
        + _ffn_scratch(ffn_params) + _staging_scratch(),
        compiler_params=pltpu.CompilerParams(
            dimension_semantics=("arbitrary", "arbitrary"), vmem_limit_bytes=V7X_VMEM_LIMIT),
        name="short_conv_mixer_swiglu_ple",
    )(x, p_all, w_in, conv_w, w_out, ln_g, ln_b, *ffn_params)


def _rope_tables(seq):
    half = HEAD_DIM // 2
    inv = ROPE_THETA ** (-np.arange(half, dtype=np.float64) / half)
    ang = np.arange(seq, dtype=np.float64)[:, None] * inv[None, :]
    cos = np.concatenate([np.cos(ang), np.cos(ang)], axis=-1)
    sin = np.concatenate([-np.sin(ang), np.sin(ang)], axis=-1)
    scale = HEAD_DIM ** -0.5 * np.log2(np.e)
    return np.stack([cos * scale, sin * scale, cos, sin]).astype(np.float32)


def kernel(x, p, even_w_in, even_w_out, conf_conv_w, conf_conv_b, conf_ln_g, conf_ln_b, odd_w_in, odd_conv_w, odd_w_out, ln_mix_g, ln_mix_b, ln_ffn_g, ln_ffn_b, ffn_w_in, ffn_w_out, ple_w_proj, ple_w_gate):
    b, s, d = x.shape
    depth = p.shape[0]
    rope_tab = _rope_tables(s)
    p_all = p.reshape(depth, b * s, PLE_DIM)
    conf_conv_b, conf_ln_g, conf_ln_b, ln_mix_g, ln_mix_b, ln_ffn_g, ln_ffn_b = (
        _rows(v) for v in (conf_conv_b, conf_ln_g, conf_ln_b, ln_mix_g, ln_mix_b, ln_ffn_g, ln_ffn_b))
    for i in range(depth):
        j = i // 2
        ffn_params = (ffn_w_in, ffn_w_out, ln_ffn_g, ln_ffn_b, ple_w_proj, ple_w_gate)
        if i % 2 == 0:
            g0, g1, g2, glu = _even_in(x, even_w_in, j, rope_tab)
            attn = _attention(g0, g1, g2)
            conv_params = (conf_conv_w, conf_conv_b, conf_ln_g, conf_ln_b)
            x = _even_tail(attn.reshape(b * s, -1), glu.reshape(b * s, -1), x.reshape(b * s, d), p_all,
                           even_w_out, conv_params, j, ln_mix_g, ln_mix_b, ffn_params, i,
                           s // ROW_TILE).reshape(b, s, d)
        else:
            x = _odd_layer(x, p_all, odd_w_in, odd_conv_w, odd_w_out, j, ln_mix_g, ln_mix_b, ffn_params, i)
    return x
```

```python
import functools

import jax
import jax.numpy as jnp
import numpy as np
from jax import lax
from jax.experimental import pallas as pl
from jax.experimental.pallas import tpu as pltpu

F32 = jnp.float32
BF16 = jnp.bfloat16

D_MODEL = 1024
HEADS = 4
HEAD_DIM = 128
DILATIONS = (1, 4, 16)
ATT_BLOCK = 128
GROUP_WIDTH = HEADS * HEAD_DIM
QK_WIDTH = len(DILATIONS) * GROUP_WIDTH
CONF_WIDTH = 512
CONF_TAPS = 31
SHORT_TAPS = 3
FFN_HIDDEN = 2816
PLE_DIM = 256
ROPE_THETA = 10000.0
NEG_INF = -1e30
LN_EPS = 1e-5
DN_ALPHA = float(4 ** 0.25)

SPLIT_STRIDE = 4
ROW_TILE = 512
FFN_CHUNK = 256
CONV_PAD = 32
CONV_ROWS = 64
SUBLANES = 8
LANES = 128
CONV_TAIL = SUBLANES
V7X_VMEM_LIMIT = 56 * 1024 * 1024


def _dot(a, b):
    return jnp.dot(a, b, preferred_element_type=F32)


def _layer_norm(v, g, b):
    mu = jnp.mean(v, axis=-1, keepdims=True)
    c = v - mu
    var = jnp.mean(c * c, axis=-1, keepdims=True)
    return c * lax.rsqrt(var + LN_EPS) * g + b


def _sigmoid(v):
    return 0.5 * jnp.tanh(0.5 * v) + 0.5


def _shift_rows_up(v, shift):
    rows = v.shape[0] - SUBLANES
    if shift == 0:
        return v[0:rows]
    tiles = [pltpu.roll(v[k:k + SUBLANES], SUBLANES - shift, 0) for k in range(0, rows + SUBLANES, SUBLANES)]
    low = lax.broadcasted_iota(jnp.int32, (SUBLANES, v.shape[1]), 0) < SUBLANES - shift
    return jnp.concatenate([jnp.where(low, a, b) for a, b in zip(tiles[:-1], tiles[1:])], axis=0)


def _resident(shape):
    return pl.BlockSpec(shape, lambda *_: (0,) * len(shape), pipeline_mode=pl.Buffered(1))


def _layer_slab(stacked, layer):
    return pl.BlockSpec((None,) + stacked.shape[1:], lambda *_: (layer, 0, 0), pipeline_mode=pl.Buffered(1))


def _rows(stacked):
    return stacked.reshape(stacked.shape[0], 1, stacked.shape[1])


STAGE_ROWS = 512
STAGE_COLS = 512
STAGE_SLOTS = 4


def _in_hbm():
    return pl.BlockSpec(memory_space=pl.ANY)


def _weight_scratch(stacked):
    return pltpu.VMEM(stacked.shape[1:], BF16)


def _staging_scratch():
    return [pltpu.VMEM((STAGE_SLOTS, STAGE_ROWS, STAGE_COLS), F32), pltpu.SemaphoreType.DMA((STAGE_SLOTS,))]


def _fetch_weights_bf16(jobs, stage_ref, sem_ref):
    ahead = STAGE_SLOTS - 1
    chunks = [(src, dst, r0, min(STAGE_ROWS, dst.shape[0] - r0), c0)
              for src, dst in jobs
              for c0 in range(0, dst.shape[1], STAGE_COLS)
              for r0 in range(0, dst.shape[0], STAGE_ROWS)]

    def copy(i):
        src, _, r0, rows, c0 = chunks[i]
        return pltpu.make_async_copy(src.at[pl.ds(r0, rows), pl.ds(c0, STAGE_COLS)],
                                     stage_ref.at[i % STAGE_SLOTS, pl.ds(0, rows), :], sem_ref.at[i % STAGE_SLOTS])

    for i in range(min(ahead, len(chunks))):
        copy(i).start()
    for i, (_, dst, r0, rows, c0) in enumerate(chunks):
        if i + ahead < len(chunks):
            copy(i + ahead).start()
        copy(i).wait()
        dst[r0:r0 + rows, c0:c0 + STAGE_COLS] = stage_ref[i % STAGE_SLOTS, 0:rows, :].astype(BF16)


def _even_in_kernel(layer, x_ref, w_hbm, rope_ref, o0_ref, o1_ref, o2_ref, glu_ref, slab_ref, w_ref,
                    stage_ref, sem_ref):
    @pl.when((pl.program_id(0) == 0) & (pl.program_id(1) == 0))
    def _():
        _fetch_weights_bf16([(w_hbm.at[layer], w_ref)], stage_ref, sem_ref)

    tm = x_ref.shape[1]
    xb = x_ref[0].astype(BF16)
    outs = (o0_ref, o1_ref, o2_ref)
    a = _dot(xb, w_ref[:, 3 * QK_WIDTH:3 * QK_WIDTH + CONF_WIDTH])
    gate = _dot(xb, w_ref[:, 3 * QK_WIDTH + CONF_WIDTH:])
    glu_ref[0] = a * _sigmoid(gate)
    for j in range(3):
        for g, dil in reversed(list(enumerate(DILATIONS))):
            c0 = j * QK_WIDTH + g * GROUP_WIDTH
            acc = _dot(xb, w_ref[:, c0:c0 + GROUP_WIDTH])
            slab, regroup = slab_ref.at[0], slab_ref.at[1]
            for h in range(HEADS):
                t = acc[:, h * HEAD_DIM:(h + 1) * HEAD_DIM]
                if j < 2:
                    t = t * rope_ref[2 * j] + pltpu.roll(t, HEAD_DIM // 2, 1) * rope_ref[2 * j + 1]
                if dil == 1:
                    o0_ref[0, j, h] = t.astype(BF16)
                else:
                    slab[h] = t
            if dil == SPLIT_STRIDE:
                for h in range(HEADS):
                    for r in range(dil):
                        outs[g][0, j, h, r] = slab[h, pl.ds(r, tm // dil, stride=dil), :].astype(BF16)
            elif dil == SPLIT_STRIDE * SPLIT_STRIDE:
                part = tm // SPLIT_STRIDE
                for h in range(HEADS):
                    for r in range(SPLIT_STRIDE):
                        regroup[h, r * part:(r + 1) * part, :] = slab[h, pl.ds(r, part, stride=SPLIT_STRIDE), :]
                for h in range(HEADS):
                    for r in range(dil):
                        lo, hi = r % SPLIT_STRIDE, r // SPLIT_STRIDE
                        rows = pl.ds(lo * part + hi, tm // dil, stride=SPLIT_STRIDE)
                        outs[g][0, j, h, r] = regroup[h, rows, :].astype(BF16)


def _even_in(x, w_in, layer, rope_tab):
    b, s, d = x.shape
    tm = ROW_TILE
    qkv_shapes = [jax.ShapeDtypeStruct((b, 3, HEADS, s, HEAD_DIM), BF16)]
    qkv_specs = [pl.BlockSpec((1, 3, HEADS, tm, HEAD_DIM), lambda bi, mi: (bi, 0, 0, mi, 0))]
    for dil in DILATIONS[1:]:
        qkv_shapes.append(jax.ShapeDtypeStruct((b, 3, HEADS, dil, s // dil, HEAD_DIM), BF16))
        qkv_specs.append(pl.BlockSpec((1, 3, HEADS, dil, tm // dil, HEAD_DIM),
                                      lambda bi, mi: (bi, 0, 0, 0, mi, 0)))
    return pl.pallas_call(
        functools.partial(_even_in_kernel, layer),
        grid=(b, s // tm),
        in_specs=[
            pl.BlockSpec((1, tm, d), lambda bi, mi: (bi, mi, 0)),
            _in_hbm(),
            pl.BlockSpec((4, tm, HEAD_DIM), lambda bi, mi: (0, mi, 0)),
        ],
        out_specs=qkv_specs + [pl.BlockSpec((1, tm, CONF_WIDTH), lambda bi, mi: (bi, mi, 0))],
        out_shape=qkv_shapes + [jax.ShapeDtypeStruct((b, s, CONF_WIDTH), F32)],
        scratch_shapes=[pltpu.VMEM((2, HEADS, tm, HEAD_DIM), F32), _weight_scratch(w_in)] + _staging_scratch(),
        compiler_params=pltpu.CompilerParams(
            dimension_semantics=("arbitrary", "arbitrary"), vmem_limit_bytes=V7X_VMEM_LIMIT),
        name="even_in_proj",
    )(x, w_in, rope_tab)


ATT_HEADS = 2
ATT_BATCH = 8


def _attn_head(head, g0_ref, g1_ref, g2_ref, o_ref, out_ref, lse_ref, stage_ref):
    seq = o_ref.shape[1]
    nb = ATT_BLOCK
    n_blocks = seq // nb
    qi = lax.broadcasted_iota(jnp.int32, (nb, 1), 0)
    kj = lax.broadcasted_iota(jnp.int32, (1, 2 * nb), 1)
    cur_only = (kj >= nb) & (kj <= qi + nb)
    band = (kj >= qi) & (kj <= qi + nb)

    for g, ref, dil in reversed(list(zip(range(3), (g0_ref, g1_ref, g2_ref), DILATIONS))):
        per_res = n_blocks // dil
        firsts = range(0, n_blocks, per_res)

        def window(j, i):
            if i in firsts:
                return ref[0, j, head, i]
            return ref[0, j, head, i - 1:i + 1].reshape(2 * nb, HEAD_DIM)

        for c0 in range(0, n_blocks, ATT_BATCH):
            blocks = list(range(c0, c0 + ATT_BATCH))
            s = [lax.dot_general(ref[0, 0, head, i], window(1, i), (((1,), (1,)), ((), ())),
                                 preferred_element_type=F32) for i in blocks]
            masks = [cur_only[:, nb:] if i in firsts else band for i in blocks]
            s = [jnp.where(mk, v, NEG_INF) for mk, v in zip(masks, s)]
            m = [jnp.max(v, axis=-1, keepdims=True) for v in s]
            ex = [jnp.exp2(v - mx) for v, mx in zip(s, m)]
            dens = [jnp.sum(e, axis=-1, keepdims=True) for e in ex]
            acc = [_dot(e.astype(BF16), window(2, i)) for e, i in zip(ex, blocks)]
            for i, a, mx, den in zip(blocks, acc, m, dens):
                out = a[:, 0:nb] / den
                lse = jnp.broadcast_to(mx + jnp.log2(den), (nb, HEAD_DIM))
                start = (i % per_res) * nb * dil + i // per_res
                if dil == SPLIT_STRIDE * SPLIT_STRIDE:
                    part, row0 = start % SPLIT_STRIDE, start // SPLIT_STRIDE
                    rows = pl.ds(row0, nb, stride=SPLIT_STRIDE)
                    stage_ref[0, part, rows, :] = out
                    stage_ref[1, part, rows, :] = lse
                else:
                    rows = pl.ds(start, nb) if dil == 1 else pl.ds(start, nb, stride=dil)
                    out_ref[g, rows, :] = out
                    lse_ref[g, rows, :] = lse
        if dil == SPLIT_STRIDE * SPLIT_STRIDE:
            for part in range(SPLIT_STRIDE):
                rows = pl.ds(part, seq // SPLIT_STRIDE, stride=SPLIT_STRIDE)
                out_ref[g, rows, :] = stage_ref[0, part]
                lse_ref[g, rows, :] = stage_ref[1, part]

    chunk = 256
    for c in range(seq // chunk):
        rows = pl.ds(c * chunk, chunk)
        lse = [lse_ref[g, rows, :] for g in range(3)]
        top = jnp.maximum(jnp.maximum(lse[0], lse[1]), lse[2])
        num = jnp.zeros((chunk, HEAD_DIM), F32)
        den = jnp.zeros((chunk, HEAD_DIM), F32)
        for g in range(3):
            a = jnp.exp2(lse[g] - top)
            num = num + a * out_ref[g, rows, :]
            den = den + a
        o_ref[0, rows, head * HEAD_DIM:(head + 1) * HEAD_DIM] = (num / den).astype(o_ref.dtype)


def _attn_kernel(g0_ref, g1_ref, g2_ref, o_ref, *scratch):
    for head in range(g0_ref.shape[2]):
        _attn_head(head, g0_ref, g1_ref, g2_ref, o_ref, *scratch)


def _attention(g0, g1, g2):
    b, _, _, s, e = g0.shape
    nb = ATT_BLOCK
    blocked = [a.reshape(b, 3, HEADS, s // nb, nb, e) for a in (g0, g1, g2)]
    spec = pl.BlockSpec((1, 3, ATT_HEADS, s // nb, nb, e), lambda bi, hi: (bi, 0, hi, 0, 0, 0))
    return pl.pallas_call(
        _attn_kernel,
        grid=(b, HEADS // ATT_HEADS),
        in_specs=[spec] * 3,
        out_specs=pl.BlockSpec((1, s, ATT_HEADS * e), lambda bi, hi: (bi, 0, hi)),
        out_shape=jax.ShapeDtypeStruct((b, s, GROUP_WIDTH), BF16),
        scratch_shapes=[pltpu.VMEM((3, s, e), F32)] * 2
        + [pltpu.VMEM((2, SPLIT_STRIDE, s // SPLIT_STRIDE, e), F32)],
        compiler_params=pltpu.CompilerParams(
            dimension_semantics=("arbitrary", "arbitrary"), vmem_limit_bytes=V7X_VMEM_LIMIT),
        name="dilated_attention",
    )(*blocked)


def _swiglu_ple(x1, p_rows, hid_ref, w_in_ref, w_out_ref, g_ref, b_ref, wp_ref, wg_ref, side_work=None):
    xb = x1.astype(BF16)
    for c in range(FFN_HIDDEN // FFN_CHUNK):
        cols = slice(c * FFN_CHUNK, (c + 1) * FFN_CHUNK)
        gate = _dot(xb, w_in_ref[:, cols])
        up = _dot(xb, w_in_ref[:, FFN_HIDDEN + c * FFN_CHUNK:FFN_HIDDEN + (c + 1) * FFN_CHUNK])
        hid_ref[:, cols] = (gate * _sigmoid(gate) * up).astype(BF16)
        if side_work is not None:
            side_work(c, gate[gate.shape[0] - SUBLANES:, 0:LANES])
    y = _dot(hid_ref[...], w_out_ref[...])
    x2 = _layer_norm(DN_ALPHA * x1 + y, g_ref[...], b_ref[...])
    emb = _dot(p_rows.astype(BF16), wp_ref[...])
    return x2 + emb * _sigmoid(_dot(x2.astype(BF16), wg_ref[...]))


def _ffn_specs(ffn_params, layer):
    w_in, w_out, ln_g, ln_b, w_proj, w_gate = ffn_params
    return [_in_hbm(), _in_hbm(), _layer_slab(ln_g, layer), _layer_slab(ln_b, layer), _in_hbm(), _in_hbm()]


def _ffn_scratch(ffn_params):
    w_in, w_out, _, _, w_proj, w_gate = ffn_params
    return [_weight_scratch(a) for a in (w_in, w_out, w_proj, w_gate)]


def _ffn_jobs(layer, ffn_in, ffn_bufs):
    w_in_hbm, w_out_hbm, g_ref, b_ref, wp_hbm, wg_hbm = ffn_in
    w_in_ref, w_out_ref, wp_ref, wg_ref = ffn_bufs
    jobs = [(w_in_hbm.at[layer], w_in_ref), (w_out_hbm.at[layer], w_out_ref),
            (wp_hbm.at[layer], wp_ref), (wg_hbm.at[layer], wg_ref)]
    return jobs, (w_in_ref, w_out_ref, g_ref, b_ref, wp_ref, wg_ref)


def _zero_from(anchor):
    bits = pltpu.bitcast(anchor, jnp.int32)
    return lax.shift_right_logical(lax.shift_right_logical(bits, 16), 16).astype(F32)[0:1, :]


def _conv_block(hist_ref, pre_ref, cw_ref, cb_ref, row_block, lane_block, after):
    first = CONV_PAD - (CONF_TAPS - 1)
    lanes = slice(lane_block * LANES, (lane_block + 1) * LANES)
    t0 = row_block * CONV_ROWS
    win = hist_ref[t0:t0 + CONV_ROWS + CONV_PAD + CONV_TAIL, lanes]
    for value in after:
        win = win + _zero_from(value)
    acc = jnp.broadcast_to(cb_ref[:, lanes], (CONV_ROWS, LANES))
    for s in range(SUBLANES):
        base, shift = divmod(first + s, SUBLANES)
        part = None
        for j in range(s, CONF_TAPS, SUBLANES):
            k0 = (j - s) + base * SUBLANES
            term = cw_ref[j:j + 1, lanes] * win[k0:k0 + CONV_ROWS + SUBLANES]
            part = term if part is None else part + term
        acc = acc + _shift_rows_up(part, shift)
    pre_ref[t0:t0 + CONV_ROWS, lanes] = acc
    return acc[0:SUBLANES]


def _conv_blocks(hist_ref, pre_ref, dst_ref, conv_refs, blocks, after):
    cw_ref, cb_ref, lg_ref, lb_ref = conv_refs
    for rb, lb in blocks:
        after = [_conv_block(hist_ref, pre_ref, cw_ref, cb_ref, rb, lb, after)]
        if lb == CONF_WIDTH // LANES - 1:
            rows = slice(rb * CONV_ROWS, (rb + 1) * CONV_ROWS)
            y = _layer_norm(pre_ref[rows, :], lg_ref[...], lb_ref[...])
            dst_ref[rows, :] = (y * _sigmoid(y)).astype(BF16)
    return after


def _even_tail_kernel(tiles_per_seq, mixer_layer, layer, attn_ref, glu0_ref, glu_next_ref, x_ref, p_ref, w_hbm,
                      mg_ref, mb_ref, cw_ref, cb_ref, lg_ref, lb_ref, *rest):
    ffn_in, rest = rest[:6], rest[6:]
    o_ref, hid_ref, hist_ref, pre_ref, conv_cur_ref, conv_next_ref, w_ref, *ffn_bufs, stage_ref, sem_ref = rest
    tm = x_ref.shape[0]
    step = pl.program_id(0)
    ffn_jobs, ffn_refs = _ffn_jobs(layer, ffn_in, ffn_bufs)

    @pl.when(step == 0)
    def _():
        _fetch_weights_bf16([(w_hbm.at[mixer_layer], w_ref)] + ffn_jobs, stage_ref, sem_ref)

    conv_refs = (cw_ref, cb_ref, lg_ref, lb_ref)
    blocks = [(rb, lb) for rb in range(tm // CONV_ROWS) for lb in range(CONF_WIDTH // LANES)]

    def stage(glu_ref, conv_tile):
        hist_ref[CONV_PAD:CONV_PAD + tm, :] = glu_ref[...]
        return (conv_tile + 1) % tiles_per_seq != 0

    def carry(keep):
        tail = hist_ref[tm:tm + CONV_PAD, :]
        hist_ref[0:CONV_PAD, :] = jnp.where(keep, tail, 0.0)

    @pl.when(step == 0)
    def _():
        hist_ref[...] = jnp.zeros(hist_ref.shape, F32)
        keep = stage(glu0_ref, step)
        zero = jnp.zeros((SUBLANES, LANES), F32)
        _conv_blocks(hist_ref, pre_ref, conv_cur_ref, conv_refs, blocks, [zero])
        carry(keep)

    mix = _dot(attn_ref[...], w_ref[0:GROUP_WIDTH, :]) + _dot(conv_cur_ref[...], w_ref[GROUP_WIDTH:, :])
    x1 = _layer_norm(DN_ALPHA * x_ref[...] + mix, mg_ref[...], mb_ref[...])

    keep = stage(glu_next_ref, step + 1)
    chunks = FFN_HIDDEN // FFN_CHUNK
    per_chunk = -(-len(blocks) // chunks)
    state = {"after": []}

    def side_work(c, anchor):
        todo = blocks[c * per_chunk:(c + 1) * per_chunk]
        state["after"] = _conv_blocks(hist_ref, pre_ref, conv_next_ref, conv_refs, todo, [anchor] + state["after"])

    o_ref[...] = _swiglu_ple(x1, p_ref[...], hid_ref, *ffn_refs, side_work=side_work)
    carry(keep)
    conv_cur_ref[...] = conv_next_ref[...]


def _even_tail(attn, glu, x2d, p_all, w_out, conv_params, mixer_layer, ln_g, ln_b, ffn_params, layer, tiles_per_seq):
    n, d = x2d.shape
    tm = ROW_TILE
    last = n // tm - 1
    return pl.pallas_call(
        functools.partial(_even_tail_kernel, tiles_per_seq, mixer_layer, layer),
        grid=(n // tm,),
        in_specs=[
            pl.BlockSpec((tm, GROUP_WIDTH), lambda i: (i, 0)),
            pl.BlockSpec((tm, CONF_WIDTH), lambda i: (0, 0)),
            pl.BlockSpec((tm, CONF_WIDTH), lambda i: (jnp.minimum(i + 1, last), 0)),
            pl.BlockSpec((tm, d), lambda i: (i, 0)),
            pl.BlockSpec((None, tm, PLE_DIM), lambda i: (layer, i, 0)),
            _in_hbm(), _layer_slab(ln_g, layer), _layer_slab(ln_b, layer),
        ] + [_layer_slab(a, mixer_layer) for a in conv_params] + _ffn_specs(ffn_params, layer),
        out_specs=pl.BlockSpec((tm, d), lambda i: (i, 0)),
        out_shape=jax.ShapeDtypeStruct((n, d), F32),
        scratch_shapes=[pltpu.VMEM((tm, FFN_HIDDEN), BF16),
                        pltpu.VMEM((CONV_PAD + tm + CONV_TAIL, CONF_WIDTH), F32),
                        pltpu.VMEM((tm, CONF_WIDTH), F32),
                        pltpu.VMEM((tm, CONF_WIDTH), BF16), pltpu.VMEM((tm, CONF_WIDTH), BF16),
                        _weight_scratch(w_out)] + _ffn_scratch(ffn_params) + _staging_scratch(),
        compiler_params=pltpu.CompilerParams(
            dimension_semantics=("arbitrary",), vmem_limit_bytes=V7X_VMEM_LIMIT),
        name="conformer_out_proj_swiglu_ple",
    )(attn, glu, glu, x2d, p_all, w_out, ln_g, ln_b, *conv_params, *ffn_params)


SHORT_CARRY = 8


def _odd_layer_kernel(mixer_layer, layer, x_ref, p_ref, w_in_hbm, cw_ref, w_out_hbm, mg_ref, mb_ref, *rest):
    ffn_in, rest = rest[:6], rest[6:]
    o_ref, gate_ref, mix_ref, hid_ref, w_in_ref, w_out_ref, *ffn_bufs, stage_ref, sem_ref = rest
    ffn_jobs, ffn_refs = _ffn_jobs(layer, ffn_in, ffn_bufs)

    @pl.when((pl.program_id(0) == 0) & (pl.program_id(1) == 0))
    def _():
        mixer_jobs = [(w_in_hbm.at[mixer_layer], w_in_ref), (w_out_hbm.at[mixer_layer], w_out_ref)]
        _fetch_weights_bf16(mixer_jobs + ffn_jobs, stage_ref, sem_ref)

    tm = x_ref.shape[1]
    width = D_MODEL
    chunk = 512

    @pl.when(pl.program_id(1) == 0)
    def _():
        gate_ref[0:SHORT_CARRY, :] = jnp.zeros((SHORT_CARRY, width), F32)

    xb = x_ref[0].astype(BF16)
    for c in range(width // chunk):
        cols = slice(c * chunk, (c + 1) * chunk)
        cg = _dot(xb, w_in_ref[:, width + c * chunk:width + (c + 1) * chunk])
        hh = _dot(xb, w_in_ref[:, 2 * width + c * chunk:2 * width + (c + 1) * chunk])
        gate_ref[SHORT_CARRY:, cols] = cg * hh
        y = jnp.zeros((tm, chunk), F32)
        for j in range(SHORT_TAPS):
            off = SHORT_CARRY - (SHORT_TAPS - 1) + j
            y = y + cw_ref[j:j + 1, cols] * gate_ref[off:off + tm, cols]
        bg = _dot(xb, w_in_ref[:, cols])
        mix_ref[:, cols] = (bg * y).astype(BF16)
        gate_ref[0:SHORT_CARRY, cols] = gate_ref[tm:tm + SHORT_CARRY, cols]
    mix = _dot(mix_ref[...], w_out_ref[...])
    x1 = _layer_norm(DN_ALPHA * x_ref[0] + mix, mg_ref[...], mb_ref[...])
    o_ref[0] = _swiglu_ple(x1, p_ref[...], hid_ref, *ffn_refs)


def _odd_layer(x, p_all, w_in, conv_w, w_out, mixer_layer, ln_g, ln_b, ffn_params, layer):
    b, s, d = x.shape
    tm = ROW_TILE
    tiles = s // tm
    return pl.pallas_call(
        functools.partial(_odd_layer_kernel, mixer_layer, layer),
        grid=(b, tiles),
        in_specs=[
            pl.BlockSpec((1, tm, d), lambda bi, mi: (bi, mi, 0)),
            pl.BlockSpec((None, tm, PLE_DIM), lambda bi, mi: (layer, bi * tiles + mi, 0)),
            _in_hbm(), _layer_slab(conv_w, mixer_layer), _in_hbm(),
            _layer_slab(ln_g, layer), _layer_slab(ln_b, layer),
        ] + _ffn_specs(ffn_params, layer),
        out_specs=pl.BlockSpec((1, tm, d), lambda bi, mi: (bi, mi, 0)),
        out_shape=jax.ShapeDtypeStruct((b, s, d), F32),
        scratch_shapes=[pltpu.VMEM((SHORT_CARRY + tm, d), F32), pltpu.VMEM((tm, d), BF16),
                        pltpu.VMEM((tm, FFN_HIDDEN), BF16), _weight_scratch(w_in), _weight_scratch(w_out)]
        + _ffn_scratch(ffn_params) + _staging_scratch(),
        compiler_params=pltpu.CompilerParams(
            dimension_semantics=("arbitrary", "arbitrary"), vmem_limit_bytes=V7X_VMEM_LIMIT),
        name="short_conv_mixer_swiglu_ple",
    )(x, p_all, w_in, conv_w, w_out, ln_g, ln_b, *ffn_params)


def _rope_tables(seq):
    half = HEAD_DIM // 2
    inv = ROPE_THETA ** (-np.arange(half, dtype=np.float64) / half)
    ang = np.arange(seq, dtype=np.float64)[:, None] * inv[None, :]
    cos = np.concatenate([np.cos(ang), np.cos(ang)], axis=-1)
    sin = np.concatenate([-np.sin(ang), np.sin(ang)], axis=-1)
    scale = HEAD_DIM ** -0.5 * np.log2(np.e)
    return np.stack([cos * scale, sin * scale, cos, sin]).astype(np.float32)


def kernel(x, p, even_w_in, even_w_out, conf_conv_w, conf_conv_b, conf_ln_g, conf_ln_b, odd_w_in, odd_conv_w, odd_w_out, ln_mix_g, ln_mix_b, ln_ffn_g, ln_ffn_b, ffn_w_in, ffn_w_out, ple_w_proj, ple_w_gate):
    b, s, d = x.shape
    depth = p.shape[0]
    rope_tab = _rope_tables(s)
    p_all = p.reshape(depth, b * s, PLE_DIM)
    conf_conv_b, conf_ln_g, conf_ln_b, ln_mix_g, ln_mix_b, ln_ffn_g, ln_ffn_b = (
        _rows(v) for v in (conf_conv_b, conf_ln_g, conf_ln_b, ln_mix_g, ln_mix_b, ln_ffn_g, ln_ffn_b))
    for i in range(depth):
        j = i // 2
        ffn_params = (ffn_w_in, ffn_w_out, ln_ffn_g, ln_ffn_b, ple_w_proj, ple_w_gate)
        if i % 2 == 0:
            g0, g1, g2, glu = _even_in(x, even_w_in, j, rope_tab)
            attn = _attention(g0, g1, g2)
            conv_params = (conf_conv_w, conf_conv_b, conf_ln_g, conf_ln_b)
            x = _even_tail(attn.reshape(b * s, -1), glu.reshape(b * s, -1), x.reshape(b * s, d), p_all,
                           even_w_out, conv_params, j, ln_mix_g, ln_mix_b, ffn_params, i,
                           s // ROW_TILE).reshape(b, s, d)
        else:
            x = _odd_layer(x, p_all, odd_w_in, odd_conv_w, odd_w_out, j, ln_mix_g, ln_mix_b, ffn_params, i)
    return x
```

```python
import functools

import jax
import jax.numpy as jnp
import numpy as np
from jax import lax
from jax.experimental import pallas as pl
from jax.experimental.pallas import tpu as pltpu

F32 = jnp.float32
BF16 = jnp.bfloat16

D_MODEL = 1024
HEADS = 4
HEAD_DIM = 128
DILATIONS = (1, 4, 16)
ATT_BLOCK = 128
GROUP_WIDTH = HEADS * HEAD_DIM
QK_WIDTH = len(DILATIONS) * GROUP_WIDTH
CONF_WIDTH = 512
CONF_TAPS = 31
SHORT_TAPS = 3
FFN_HIDDEN = 2816
PLE_DIM = 256
ROPE_THETA = 10000.0
NEG_INF = -1e30
LN_EPS = 1e-5
DN_ALPHA = float(4 ** 0.25)

SPLIT_STRIDE = 4
ROPE_ROWS = 64
ROW_TILE = 512
FFN_CHUNK = 256
CONV_PAD = 32
CONV_ROWS = 64
SUBLANES = 8
LANES = 128
CONV_TAIL = SUBLANES
V7X_VMEM_LIMIT = 56 * 1024 * 1024


def _dot(a, b):
    return jnp.dot(a, b, preferred_element_type=F32)


def _layer_norm(v, g, b):
    mu = jnp.mean(v, axis=-1, keepdims=True)
    c = v - mu
    var = jnp.mean(c * c, axis=-1, keepdims=True)
    return c * lax.rsqrt(var + LN_EPS) * g + b


def _sigmoid(v):
    return 0.5 * jnp.tanh(0.5 * v) + 0.5


def _shift_rows_up(v, shift):
    rows = v.shape[0] - SUBLANES
    if shift == 0:
        return v[0:rows]
    tiles = [pltpu.roll(v[k:k + SUBLANES], SUBLANES - shift, 0) for k in range(0, rows + SUBLANES, SUBLANES)]
    low = lax.broadcasted_iota(jnp.int32, (SUBLANES, v.shape[1]), 0) < SUBLANES - shift
    return jnp.concatenate([jnp.where(low, a, b) for a, b in zip(tiles[:-1], tiles[1:])], axis=0)


def _resident(shape):
    return pl.BlockSpec(shape, lambda *_: (0,) * len(shape), pipeline_mode=pl.Buffered(1))


def _layer_slab(stacked, layer):
    return pl.BlockSpec((None,) + stacked.shape[1:], lambda *_: (layer, 0, 0), pipeline_mode=pl.Buffered(1))


def _rows(stacked):
    return stacked.reshape(stacked.shape[0], 1, stacked.shape[1])


STAGE_ROWS = 512
STAGE_COLS = 512
STAGE_SLOTS = 4


def _in_hbm():
    return pl.BlockSpec(memory_space=pl.ANY)


def _weight_scratch(stacked):
    return pltpu.VMEM(stacked.shape[1:], BF16)


def _staging_scratch():
    return [pltpu.VMEM((STAGE_SLOTS, STAGE_ROWS, STAGE_COLS), F32), pltpu.SemaphoreType.DMA((STAGE_SLOTS,))]


def _fetch_weights_bf16(jobs, stage_ref, sem_ref):
    ahead = STAGE_SLOTS - 1
    chunks = [(src, dst, r0, min(STAGE_ROWS, dst.shape[0] - r0), c0)
              for src, dst in jobs
              for c0 in range(0, dst.shape[1], STAGE_COLS)
              for r0 in range(0, dst.shape[0], STAGE_ROWS)]

    def copy(i):
        src, _, r0, rows, c0 = chunks[i]
        return pltpu.make_async_copy(src.at[pl.ds(r0, rows), pl.ds(c0, STAGE_COLS)],
                                     stage_ref.at[i % STAGE_SLOTS, pl.ds(0, rows), :], sem_ref.at[i % STAGE_SLOTS])

    for i in range(min(ahead, len(chunks))):
        copy(i).start()
    for i, (_, dst, r0, rows, c0) in enumerate(chunks):
        if i + ahead < len(chunks):
            copy(i + ahead).start()
        copy(i).wait()
        dst[r0:r0 + rows, c0:c0 + STAGE_COLS] = stage_ref[i % STAGE_SLOTS, 0:rows, :].astype(BF16)


def _even_in_kernel(layer, x_ref, w_hbm, rope_ref, o0_ref, o1_ref, o2_ref, glu_ref, slab_ref, w_ref,
                    stage_ref, sem_ref):
    @pl.when((pl.program_id(0) == 0) & (pl.program_id(1) == 0))
    def _():
        _fetch_weights_bf16([(w_hbm.at[layer], w_ref)], stage_ref, sem_ref)

    tm = x_ref.shape[1]
    xb = x_ref[0].astype(BF16)
    outs = (o0_ref, o1_ref, o2_ref)
    a = _dot(xb, w_ref[:, 3 * QK_WIDTH:3 * QK_WIDTH + CONF_WIDTH])
    gate = _dot(xb, w_ref[:, 3 * QK_WIDTH + CONF_WIDTH:])
    glu_ref[0] = a * _sigmoid(gate)
    for j in range(3):
        for g, dil in reversed(list(enumerate(DILATIONS))):
            c0 = j * QK_WIDTH + g * GROUP_WIDTH
            acc = _dot(xb, w_ref[:, c0:c0 + GROUP_WIDTH])
            slab, regroup = slab_ref.at[0], slab_ref.at[1]
            for p0 in range(0, tm, ROPE_ROWS):
                rows = slice(p0, p0 + ROPE_ROWS)
                if j < 2:
                    cos, sin = rope_ref[2 * j, rows, :], rope_ref[2 * j + 1, rows, :]
                for h in range(HEADS):
                    t = acc[rows, h * HEAD_DIM:(h + 1) * HEAD_DIM]
                    if j < 2:
                        t = t * cos + pltpu.roll(t, HEAD_DIM // 2, 1) * sin
                    if dil == 1:
                        o0_ref[0, j, h, rows, :] = t.astype(BF16)
                    else:
                        slab[h, rows, :] = t
            if dil == SPLIT_STRIDE:
                for h in range(HEADS):
                    for r in range(dil):
                        outs[g][0, j, h, r] = slab[h, pl.ds(r, tm // dil, stride=dil), :].astype(BF16)
            elif dil == SPLIT_STRIDE * SPLIT_STRIDE:
                part = tm // SPLIT_STRIDE
                for h in range(HEADS):
                    for r in range(SPLIT_STRIDE):
                        regroup[h, r * part:(r + 1) * part, :] = slab[h, pl.ds(r, part, stride=SPLIT_STRIDE), :]
                for h in range(HEADS):
                    for r in range(dil):
                        lo, hi = r % SPLIT_STRIDE, r // SPLIT_STRIDE
                        rows = pl.ds(lo * part + hi, tm // dil, stride=SPLIT_STRIDE)
                        outs[g][0, j, h, r] = regroup[h, rows, :].astype(BF16)


def _even_in(x, w_in, layer, rope_tab):
    b, s, d = x.shape
    tm = ROW_TILE
    qkv_shapes = [jax.ShapeDtypeStruct((b, 3, HEADS, s, HEAD_DIM), BF16)]
    qkv_specs = [pl.BlockSpec((1, 3, HEADS, tm, HEAD_DIM), lambda bi, mi: (bi, 0, 0, mi, 0))]
    for dil in DILATIONS[1:]:
        qkv_shapes.append(jax.ShapeDtypeStruct((b, 3, HEADS, dil, s // dil, HEAD_DIM), BF16))
        qkv_specs.append(pl.BlockSpec((1, 3, HEADS, dil, tm // dil, HEAD_DIM),
                                      lambda bi, mi: (bi, 0, 0, 0, mi, 0)))
    return pl.pallas_call(
        functools.partial(_even_in_kernel, layer),
        grid=(b, s // tm),
        in_specs=[
            pl.BlockSpec((1, tm, d), lambda bi, mi: (bi, mi, 0)),
            _in_hbm(),
            pl.BlockSpec((4, tm, HEAD_DIM), lambda bi, mi: (0, mi, 0)),
        ],
        out_specs=qkv_specs + [pl.BlockSpec((1, tm, CONF_WIDTH), lambda bi, mi: (bi, mi, 0))],
        out_shape=qkv_shapes + [jax.ShapeDtypeStruct((b, s, CONF_WIDTH), F32)],
        scratch_shapes=[pltpu.VMEM((2, HEADS, tm, HEAD_DIM), F32), _weight_scratch(w_in)] + _staging_scratch(),
        compiler_params=pltpu.CompilerParams(
            dimension_semantics=("arbitrary", "arbitrary"), vmem_limit_bytes=V7X_VMEM_LIMIT),
        name="even_in_proj",
    )(x, w_in, rope_tab)


ATT_HEADS = 2
ATT_BATCH = 8


def _attn_head(head, g0_ref, g1_ref, g2_ref, o_ref, out_ref, lse_ref, stage_ref):
    seq = o_ref.shape[1]
    nb = ATT_BLOCK
    n_blocks = seq // nb
    qi = lax.broadcasted_iota(jnp.int32, (nb, 1), 0)
    kj = lax.broadcasted_iota(jnp.int32, (1, 2 * nb), 1)
    cur_only = (kj >= nb) & (kj <= qi + nb)
    band = (kj >= qi) & (kj <= qi + nb)

    for g, ref, dil in reversed(list(zip(range(3), (g0_ref, g1_ref, g2_ref), DILATIONS))):
        per_res = n_blocks // dil
        firsts = range(0, n_blocks, per_res)

        def window(j, i):
            if i in firsts:
                return ref[0, j, head, i]
            return ref[0, j, head, i - 1:i + 1].reshape(2 * nb, HEAD_DIM)

        for c0 in range(0, n_blocks, ATT_BATCH):
            blocks = list(range(c0, c0 + ATT_BATCH))
            s = [lax.dot_general(ref[0, 0, head, i], window(1, i), (((1,), (1,)), ((), ())),
                                 preferred_element_type=F32) for i in blocks]
            masks = [cur_only[:, nb:] if i in firsts else band for i in blocks]
            s = [jnp.where(mk, v, NEG_INF) for mk, v in zip(masks, s)]
            m = [jnp.max(v, axis=-1, keepdims=True) for v in s]
            ex = [jnp.exp2(v - mx) for v, mx in zip(s, m)]
            dens = [jnp.sum(e, axis=-1, keepdims=True) for e in ex]
            acc = [_dot(e.astype(BF16), window(2, i)) for e, i in zip(ex, blocks)]
            for i, a, mx, den in zip(blocks, acc, m, dens):
                out = a[:, 0:nb] / den
                lse = jnp.broadcast_to(mx + jnp.log2(den), (nb, HEAD_DIM))
                start = (i % per_res) * nb * dil + i // per_res
                if dil == SPLIT_STRIDE * SPLIT_STRIDE:
                    part, row0 = start % SPLIT_STRIDE, start // SPLIT_STRIDE
                    rows = pl.ds(row0, nb, stride=SPLIT_STRIDE)
                    stage_ref[0, part, rows, :] = out
                    stage_ref[1, part, rows, :] = lse
                else:
                    rows = pl.ds(start, nb) if dil == 1 else pl.ds(start, nb, stride=dil)
                    out_ref[g, rows, :] = out
                    lse_ref[g, rows, :] = lse
        if dil == SPLIT_STRIDE * SPLIT_STRIDE:
            for part in range(SPLIT_STRIDE):
                rows = pl.ds(part, seq // SPLIT_STRIDE, stride=SPLIT_STRIDE)
                out_ref[g, rows, :] = stage_ref[0, part]
                lse_ref[g, rows, :] = stage_ref[1, part]

    chunk = 256
    for c in range(seq // chunk):
        rows = pl.ds(c * chunk, chunk)
        lse = [lse_ref[g, rows, :] for g in range(3)]
        top = jnp.maximum(jnp.maximum(lse[0], lse[1]), lse[2])
        num = jnp.zeros((chunk, HEAD_DIM), F32)
        den = jnp.zeros((chunk, HEAD_DIM), F32)
        for g in range(3):
            a = jnp.exp2(lse[g] - top)
            num = num + a * out_ref[g, rows, :]
            den = den + a
        o_ref[0, rows, head * HEAD_DIM:(head + 1) * HEAD_DIM] = (num / den).astype(o_ref.dtype)


def _attn_kernel(g0_ref, g1_ref, g2_ref, o_ref, *scratch):
    for head in range(g0_ref.shape[2]):
        _attn_head(head, g0_ref, g1_ref, g2_ref, o_ref, *scratch)


def _attention(g0, g1, g2):
    b, _, _, s, e = g0.shape
    nb = ATT_BLOCK
    blocked = [a.reshape(b, 3, HEADS, s // nb, nb, e) for a in (g0, g1, g2)]
    spec = pl.BlockSpec((1, 3, ATT_HEADS, s // nb, nb, e), lambda bi, hi: (bi, 0, hi, 0, 0, 0))
    return pl.pallas_call(
        _attn_kernel,
        grid=(b, HEADS // ATT_HEADS),
        in_specs=[spec] * 3,
        out_specs=pl.BlockSpec((1, s, ATT_HEADS * e), lambda bi, hi: (bi, 0, hi)),
        out_shape=jax.ShapeDtypeStruct((b, s, GROUP_WIDTH), BF16),
        scratch_shapes=[pltpu.VMEM((3, s, e), F32)] * 2
        + [pltpu.VMEM((2, SPLIT_STRIDE, s // SPLIT_STRIDE, e), F32)],
        compiler_params=pltpu.CompilerParams(
            dimension_semantics=("arbitrary", "arbitrary"), vmem_limit_bytes=V7X_VMEM_LIMIT),
        name="dilated_attention",
    )(*blocked)


def _swiglu_ple(x1, p_rows, hid_ref, w_in_ref, w_out_ref, g_ref, b_ref, wp_ref, wg_ref, side_work=None):
    xb = x1.astype(BF16)
    for c in range(FFN_HIDDEN // FFN_CHUNK):
        cols = slice(c * FFN_CHUNK, (c + 1) * FFN_CHUNK)
        gate = _dot(xb, w_in_ref[:, cols])
        up = _dot(xb, w_in_ref[:, FFN_HIDDEN + c * FFN_CHUNK:FFN_HIDDEN + (c + 1) * FFN_CHUNK])
        hid_ref[:, cols] = (gate * _sigmoid(gate) * up).astype(BF16)
        if side_work is not None:
            side_work(c, gate[gate.shape[0] - SUBLANES:, 0:LANES])
    y = _dot(hid_ref[...], w_out_ref[...])
    x2 = _layer_norm(DN_ALPHA * x1 + y, g_ref[...], b_ref[...])
    emb = _dot(p_rows.astype(BF16), wp_ref[...])
    return x2 + emb * _sigmoid(_dot(x2.astype(BF16), wg_ref[...]))


def _ffn_specs(ffn_params, layer):
    w_in, w_out, ln_g, ln_b, w_proj, w_gate = ffn_params
    return [_in_hbm(), _in_hbm(), _layer_slab(ln_g, layer), _layer_slab(ln_b, layer), _in_hbm(), _in_hbm()]


def _ffn_scratch(ffn_params):
    w_in, w_out, _, _, w_proj, w_gate = ffn_params
    return [_weight_scratch(a) for a in (w_in, w_out, w_proj, w_gate)]


def _ffn_jobs(layer, ffn_in, ffn_bufs):
    w_in_hbm, w_out_hbm, g_ref, b_ref, wp_hbm, wg_hbm = ffn_in
    w_in_ref, w_out_ref, wp_ref, wg_ref = ffn_bufs
    jobs = [(w_in_hbm.at[layer], w_in_ref), (w_out_hbm.at[layer], w_out_ref),
            (wp_hbm.at[layer], wp_ref), (wg_hbm.at[layer], wg_ref)]
    return jobs, (w_in_ref, w_out_ref, g_ref, b_ref, wp_ref, wg_ref)


def _zero_from(anchor):
    bits = pltpu.bitcast(anchor, jnp.int32)
    return lax.shift_right_logical(lax.shift_right_logical(bits, 16), 16).astype(F32)[0:1, :]


def _conv_block(hist_ref, pre_ref, cw_ref, cb_ref, row_block, lane_block, after):
    first = CONV_PAD - (CONF_TAPS - 1)
    lanes = slice(lane_block * LANES, (lane_block + 1) * LANES)
    t0 = row_block * CONV_ROWS
    win = hist_ref[t0:t0 + CONV_ROWS + CONV_PAD + CONV_TAIL, lanes]
    for value in after:
        win = win + _zero_from(value)
    acc = jnp.broadcast_to(cb_ref[:, lanes], (CONV_ROWS, LANES))
    for s in range(SUBLANES):
        base, shift = divmod(first + s, SUBLANES)
        part = None
        for j in range(s, CONF_TAPS, SUBLANES):
            k0 = (j - s) + base * SUBLANES
            term = cw_ref[j:j + 1, lanes] * win[k0:k0 + CONV_ROWS + SUBLANES]
            part = term if part is None else part + term
        acc = acc + _shift_rows_up(part, shift)
    pre_ref[t0:t0 + CONV_ROWS, lanes] = acc
    return acc[0:SUBLANES]


def _conv_blocks(hist_ref, pre_ref, dst_ref, conv_refs, blocks, after):
    cw_ref, cb_ref, lg_ref, lb_ref = conv_refs
    for rb, lb in blocks:
        after = [_conv_block(hist_ref, pre_ref, cw_ref, cb_ref, rb, lb, after)]
        if lb == CONF_WIDTH // LANES - 1:
            rows = slice(rb * CONV_ROWS, (rb + 1) * CONV_ROWS)
            y = _layer_norm(pre_ref[rows, :], lg_ref[...], lb_ref[...])
            dst_ref[rows, :] = (y * _sigmoid(y)).astype(BF16)
    return after


def _even_tail_kernel(tiles_per_seq, mixer_layer, layer, attn_ref, glu0_ref, glu_next_ref, x_ref, p_ref, w_hbm,
                      mg_ref, mb_ref, cw_ref, cb_ref, lg_ref, lb_ref, *rest):
    ffn_in, rest = rest[:6], rest[6:]
    o_ref, hid_ref, hist_ref, pre_ref, conv_cur_ref, conv_next_ref, w_ref, *ffn_bufs, stage_ref, sem_ref = rest
    tm = x_ref.shape[0]
    step = pl.program_id(0)
    ffn_jobs, ffn_refs = _ffn_jobs(layer, ffn_in, ffn_bufs)

    @pl.when(step == 0)
    def _():
        _fetch_weights_bf16([(w_hbm.at[mixer_layer], w_ref)] + ffn_jobs, stage_ref, sem_ref)

    conv_refs = (cw_ref, cb_ref, lg_ref, lb_ref)
    blocks = [(rb, lb) for rb in range(tm // CONV_ROWS) for lb in range(CONF_WIDTH // LANES)]

    def stage(glu_ref, conv_tile):
        hist_ref[CONV_PAD:CONV_PAD + tm, :] = glu_ref[...]
        return (conv_tile + 1) % tiles_per_seq != 0

    def carry(keep):
        tail = hist_ref[tm:tm + CONV_PAD, :]
        hist_ref[0:CONV_PAD, :] = jnp.where(keep, tail, 0.0)

    @pl.when(step == 0)
    def _():
        hist_ref[...] = jnp.zeros(hist_ref.shape, F32)
        keep = stage(glu0_ref, step)
        zero = jnp.zeros((SUBLANES, LANES), F32)
        _conv_blocks(hist_ref, pre_ref, conv_cur_ref, conv_refs, blocks, [zero])
        carry(keep)

    mix = _dot(attn_ref[...], w_ref[0:GROUP_WIDTH, :]) + _dot(conv_cur_ref[...], w_ref[GROUP_WIDTH:, :])
    x1 = _layer_norm(DN_ALPHA * x_ref[...] + mix, mg_ref[...], mb_ref[...])

    keep = stage(glu_next_ref, step + 1)
    chunks = FFN_HIDDEN // FFN_CHUNK
    per_chunk = -(-len(blocks) // chunks)
    state = {"after": []}

    def side_work(c, anchor):
        todo = blocks[c * per_chunk:(c + 1) * per_chunk]
        state["after"] = _conv_blocks(hist_ref, pre_ref, conv_next_ref, conv_refs, todo, [anchor] + state["after"])

    o_ref[...] = _swiglu_ple(x1, p_ref[...], hid_ref, *ffn_refs, side_work=side_work)
    carry(keep)
    conv_cur_ref[...] = conv_next_ref[...]


def _even_tail(attn, glu, x2d, p_all, w_out, conv_params, mixer_layer, ln_g, ln_b, ffn_params, layer, tiles_per_seq):
    n, d = x2d.shape
    tm = ROW_TILE
    last = n // tm - 1
    return pl.pallas_call(
        functools.partial(_even_tail_kernel, tiles_per_seq, mixer_layer, layer),
        grid=(n // tm,),
        in_specs=[
            pl.BlockSpec((tm, GROUP_WIDTH), lambda i: (i, 0)),
            pl.BlockSpec((tm, CONF_WIDTH), lambda i: (0, 0)),
            pl.BlockSpec((tm, CONF_WIDTH), lambda i: (jnp.minimum(i + 1, last), 0)),
            pl.BlockSpec((tm, d), lambda i: (i, 0)),
            pl.BlockSpec((None, tm, PLE_DIM), lambda i: (layer, i, 0)),
            _in_hbm(), _layer_slab(ln_g, layer), _layer_slab(ln_b, layer),
        ] + [_layer_slab(a, mixer_layer) for a in conv_params] + _ffn_specs(ffn_params, layer),
        out_specs=pl.BlockSpec((tm, d), lambda i: (i, 0)),
        out_shape=jax.ShapeDtypeStruct((n, d), F32),
        scratch_shapes=[pltpu.VMEM((tm, FFN_HIDDEN), BF16),
                        pltpu.VMEM((CONV_PAD + tm + CONV_TAIL, CONF_WIDTH), F32),
                        pltpu.VMEM((tm, CONF_WIDTH), F32),
                        pltpu.VMEM((tm, CONF_WIDTH), BF16), pltpu.VMEM((tm, CONF_WIDTH), BF16),
                        _weight_scratch(w_out)] + _ffn_scratch(ffn_params) + _staging_scratch(),
        compiler_params=pltpu.CompilerParams(
            dimension_semantics=("arbitrary",), vmem_limit_bytes=V7X_VMEM_LIMIT),
        name="conformer_out_proj_swiglu_ple",
    )(attn, glu, glu, x2d, p_all, w_out, ln_g, ln_b, *conv_params, *ffn_params)


SHORT_CARRY = 8


def _odd_layer_kernel(mixer_layer, layer, x_ref, p_ref, w_in_hbm, cw_ref, w_out_hbm, mg_ref, mb_ref, *rest):
    ffn_in, rest = rest[:6], rest[6:]
    o_ref, gate_ref, mix_ref, hid_ref, w_in_ref, w_out_ref, *ffn_bufs, stage_ref, sem_ref = rest
    ffn_jobs, ffn_refs = _ffn_jobs(layer, ffn_in, ffn_bufs)

    @pl.when((pl.program_id(0) == 0) & (pl.program_id(1) == 0))
    def _():
        mixer_jobs = [(w_in_hbm.at[mixer_layer], w_in_ref), (w_out_hbm.at[mixer_layer], w_out_ref)]
        _fetch_weights_bf16(mixer_jobs + ffn_jobs, stage_ref, sem_ref)

    tm = x_ref.shape[1]
    width = D_MODEL
    chunk = 512

    @pl.when(pl.program_id(1) == 0)
    def _():
        gate_ref[0:SHORT_CARRY, :] = jnp.zeros((SHORT_CARRY, width), F32)

    xb = x_ref[0].astype(BF16)
    for c in range(width // chunk):
        cols = slice(c * chunk, (c + 1) * chunk)
        cg = _dot(xb, w_in_ref[:, width + c * chunk:width + (c + 1) * chunk])
        hh = _dot(xb, w_in_ref[:, 2 * width + c * chunk:2 * width + (c + 1) * chunk])
        gate_ref[SHORT_CARRY:, cols] = cg * hh
        y = jnp.zeros((tm, chunk), F32)
        for j in range(SHORT_TAPS):
            off = SHORT_CARRY - (SHORT_TAPS - 1) + j
            y = y + cw_ref[j:j + 1, cols] * gate_ref[off:off + tm, cols]
        bg = _dot(xb, w_in_ref[:, cols])
        mix_ref[:, cols] = (bg * y).astype(BF16)
        gate_ref[0:SHORT_CARRY, cols] = gate_ref[tm:tm + SHORT_CARRY, cols]
    mix = _dot(mix_ref[...], w_out_ref[...])
    x1 = _layer_norm(DN_ALPHA * x_ref[0] + mix, mg_ref[...], mb_ref[...])
    o_ref[0] = _swiglu_ple(x1, p_ref[...], hid_ref, *ffn_refs)


def _odd_layer(x, p_all, w_in, conv_w, w_out, mixer_layer, ln_g, ln_b, ffn_params, layer):
    b, s, d = x.shape
    tm = ROW_TILE
    tiles = s // tm
    return pl.pallas_call(
        functools.partial(_odd_layer_kernel, mixer_layer, layer),
        grid=(b, tiles),
        in_specs=[
            pl.BlockSpec((1, tm, d), lambda bi, mi: (bi, mi, 0)),
            pl.BlockSpec((None, tm, PLE_DIM), lambda bi, mi: (layer, bi * tiles + mi, 0)),
            _in_hbm(), _layer_slab(conv_w, mixer_layer), _in_hbm(),
            _layer_slab(ln_g, layer), _layer_slab(ln_b, layer),
        ] + _ffn_specs(ffn_params, layer),
        out_specs=pl.BlockSpec((1, tm, d), lambda bi, mi: (bi, mi, 0)),
        out_shape=jax.ShapeDtypeStruct((b, s, d), F32),
        scratch_shapes=[pltpu.VMEM((SHORT_CARRY + tm, d), F32), pltpu.VMEM((tm, d), BF16),
                        pltpu.VMEM((tm, FFN_HIDDEN), BF16), _weight_scratch(w_in), _weight_scratch(w_out)]
        + _ffn_scratch(ffn_params) + _staging_scratch(),
        compiler_params=pltpu.CompilerParams(
            dimension_semantics=("arbitrary", "arbitrary"), vmem_limit_bytes=V7X_VMEM_LIMIT),
        name="short_conv_mixer_swiglu_ple",
    )(x, p_all, w_in, conv_w, w_out, ln_g, ln_b, *ffn_params)


def _rope_tables(seq):
    half = HEAD_DIM // 2
    inv = ROPE_THETA ** (-np.arange(half, dtype=np.float64) / half)
    ang = np.arange(seq, dtype=np.float64)[:, None] * inv[None, :]
    cos = np.concatenate([np.cos(ang), np.cos(ang)], axis=-1)
    sin = np.concatenate([-np.sin(ang), np.sin(ang)], axis=-1)
    scale = HEAD_DIM ** -0.5 * np.log2(np.e)
    return np.stack([cos * scale, sin * scale, cos, sin]).astype(np.float32)


def kernel(x, p, even_w_in, even_w_out, conf_conv_w, conf_conv_b, conf_ln_g, conf_ln_b, odd_w_in, odd_conv_w, odd_w_out, ln_mix_g, ln_mix_b, ln_ffn_g, ln_ffn_b, ffn_w_in, ffn_w_out, ple_w_proj, ple_w_gate):
    b, s, d = x.shape
    depth = p.shape[0]
    rope_tab = _rope_tables(s)
    p_all = p.reshape(depth, b * s, PLE_DIM)
    conf_conv_b, conf_ln_g, conf_ln_b, ln_mix_g, ln_mix_b, ln_ffn_g, ln_ffn_b = (
        _rows(v) for v in (conf_conv_b, conf_ln_g, conf_ln_b, ln_mix_g, ln_mix_b, ln_ffn_g, ln_ffn_b))
    for i in range(depth):
        j = i // 2
        ffn_params = (ffn_w_in, ffn_w_out, ln_ffn_g, ln_ffn_b, ple_w_proj, ple_w_gate)
        if i % 2 == 0:
            g0, g1, g2, glu = _even_in(x, even_w_in, j, rope_tab)
            attn = _attention(g0, g1, g2)
            conv_params = (conf_conv_w, conf_conv_b, conf_ln_g, conf_ln_b)
            x = _even_tail(attn.reshape(b * s, -1), glu.reshape(b * s, -1), x.reshape(b * s, d), p_all,
                           even_w_out, conv_params, j, ln_mix_g, ln_mix_b, ffn_params, i,
                           s // ROW_TILE).reshape(b, s, d)
        else:
            x = _odd_layer(x, p_all, odd_w_in, odd_conv_w, odd_w_out, j, ln_mix_g, ln_mix_b, ffn_params, i)
    return x
```

```python
import functools

import jax
import jax.numpy as jnp
import numpy as np
from jax import lax
from jax.experimental import pallas as pl
from jax.experimental.pallas import tpu as pltpu

F32 = jnp.float32
BF16 = jnp.bfloat16

D_MODEL = 1024
HEADS = 4
HEAD_DIM = 128
DILATIONS = (1, 4, 16)
ATT_BLOCK = 128
GROUP_WIDTH = HEADS * HEAD_DIM
QK_WIDTH = len(DILATIONS) * GROUP_WIDTH
CONF_WIDTH = 512
CONF_TAPS = 31
SHORT_TAPS = 3
FFN_HIDDEN = 2816
PLE_DIM = 256
ROPE_THETA = 10000.0
NEG_INF = -1e30
LN_EPS = 1e-5
DN_ALPHA = float(4 ** 0.25)

SPLIT_STRIDE = 4
ROW_TILE = 512
FFN_CHUNK = 256
CONV_PAD = 32
CONV_ROWS = 64
SUBLANES = 8
LANES = 128
CONV_TAIL = SUBLANES
V7X_VMEM_LIMIT = 56 * 1024 * 1024


def _dot(a, b):
    return jnp.dot(a, b, preferred_element_type=F32)


def _layer_norm(v, g, b):
    mu = jnp.mean(v, axis=-1, keepdims=True)
    c = v - mu
    var = jnp.mean(c * c, axis=-1, keepdims=True)
    return c * lax.rsqrt(var + LN_EPS) * g + b


def _sigmoid(v):
    return 0.5 * jnp.tanh(0.5 * v) + 0.5


def _shift_rows_up(v, shift):
    rows = v.shape[0] - SUBLANES
    if shift == 0:
        return v[0:rows]
    tiles = [pltpu.roll(v[k:k + SUBLANES], SUBLANES - shift, 0) for k in range(0, rows + SUBLANES, SUBLANES)]
    low = lax.broadcasted_iota(jnp.int32, (SUBLANES, v.shape[1]), 0) < SUBLANES - shift
    return jnp.concatenate([jnp.where(low, a, b) for a, b in zip(tiles[:-1], tiles[1:])], axis=0)


def _layer_slab(stacked, layer):
    return pl.BlockSpec((None,) + stacked.shape[1:], lambda *_: (layer, 0, 0), pipeline_mode=pl.Buffered(1))


def _rows(stacked):
    return stacked.reshape(stacked.shape[0], 1, stacked.shape[1])


STAGE_ROWS = 512
STAGE_COLS = 512
STAGE_SLOTS = 6


def _in_hbm():
    return pl.BlockSpec(memory_space=pl.ANY)


def _weight_scratch(stacked):
    return pltpu.VMEM(stacked.shape[1:], BF16)


def _staging_scratch():
    return [pltpu.VMEM((STAGE_SLOTS, STAGE_ROWS, STAGE_COLS), F32), pltpu.SemaphoreType.DMA((STAGE_SLOTS,))]


def _fetch_weights_bf16(jobs, stage_ref, sem_ref):
    ahead = STAGE_SLOTS - 1
    chunks = [(src, dst, r0, min(STAGE_ROWS, dst.shape[0] - r0), c0)
              for src, dst in jobs
              for c0 in range(0, dst.shape[1], STAGE_COLS)
              for r0 in range(0, dst.shape[0], STAGE_ROWS)]

    def copy(i):
        src, _, r0, rows, c0 = chunks[i]
        return pltpu.make_async_copy(src.at[pl.ds(r0, rows), pl.ds(c0, STAGE_COLS)],
                                     stage_ref.at[i % STAGE_SLOTS, pl.ds(0, rows), :], sem_ref.at[i % STAGE_SLOTS])

    for i in range(min(ahead, len(chunks))):
        copy(i).start()
    for i, (_, dst, r0, rows, c0) in enumerate(chunks):
        if i + ahead < len(chunks):
            copy(i + ahead).start()
        copy(i).wait()
        dst[r0:r0 + rows, c0:c0 + STAGE_COLS] = stage_ref[i % STAGE_SLOTS, 0:rows, :].astype(BF16)


def _even_in_kernel(layer, x_ref, w_hbm, rope_ref, o0_ref, o1_ref, o2_ref, glu_ref, slab_ref, w_ref,
                    stage_ref, sem_ref):
    @pl.when((pl.program_id(0) == 0) & (pl.program_id(1) == 0))
    def _():
        _fetch_weights_bf16([(w_hbm.at[layer], w_ref)], stage_ref, sem_ref)

    tm = x_ref.shape[1]
    xb = x_ref[0].astype(BF16)
    outs = (o0_ref, o1_ref, o2_ref)
    a = _dot(xb, w_ref[:, 3 * QK_WIDTH:3 * QK_WIDTH + CONF_WIDTH])
    gate = _dot(xb, w_ref[:, 3 * QK_WIDTH + CONF_WIDTH:])
    glu_ref[0] = a * _sigmoid(gate)
    for j in range(3):
        for g, dil in reversed(list(enumerate(DILATIONS))):
            c0 = j * QK_WIDTH + g * GROUP_WIDTH
            acc = _dot(xb, w_ref[:, c0:c0 + GROUP_WIDTH])
            slab, regroup = slab_ref.at[0], slab_ref.at[1]
            for h in range(HEADS):
                t = acc[:, h * HEAD_DIM:(h + 1) * HEAD_DIM]
                if j < 2:
                    t = t * rope_ref[2 * j] + pltpu.roll(t, HEAD_DIM // 2, 1) * rope_ref[2 * j + 1]
                if dil == 1:
                    o0_ref[0, j, h] = t.astype(BF16)
                else:
                    slab[h] = t
            if dil == SPLIT_STRIDE:
                for h in range(HEADS):
                    for r in range(dil):
                        outs[g][0, j, h, r] = slab[h, pl.ds(r, tm // dil, stride=dil), :].astype(BF16)
            elif dil == SPLIT_STRIDE * SPLIT_STRIDE:
                part = tm // SPLIT_STRIDE
                for h in range(HEADS):
                    for r in range(SPLIT_STRIDE):
                        regroup[h, r * part:(r + 1) * part, :] = slab[h, pl.ds(r, part, stride=SPLIT_STRIDE), :]
                for h in range(HEADS):
                    for r in range(dil):
                        lo, hi = r % SPLIT_STRIDE, r // SPLIT_STRIDE
                        rows = pl.ds(lo * part + hi, tm // dil, stride=SPLIT_STRIDE)
                        outs[g][0, j, h, r] = regroup[h, rows, :].astype(BF16)


def _even_in(x, w_in, layer, rope_tab):
    b, s, d = x.shape
    tm = ROW_TILE
    qkv_shapes = [jax.ShapeDtypeStruct((b, 3, HEADS, s, HEAD_DIM), BF16)]
    qkv_specs = [pl.BlockSpec((1, 3, HEADS, tm, HEAD_DIM), lambda bi, mi: (bi, 0, 0, mi, 0))]
    for dil in DILATIONS[1:]:
        qkv_shapes.append(jax.ShapeDtypeStruct((b, 3, HEADS, dil, s // dil, HEAD_DIM), BF16))
        qkv_specs.append(pl.BlockSpec((1, 3, HEADS, dil, tm // dil, HEAD_DIM),
                                      lambda bi, mi: (bi, 0, 0, 0, mi, 0)))
    return pl.pallas_call(
        functools.partial(_even_in_kernel, layer),
        grid=(b, s // tm),
        in_specs=[
            pl.BlockSpec((1, tm, d), lambda bi, mi: (bi, mi, 0)),
            _in_hbm(),
            pl.BlockSpec((4, tm, HEAD_DIM), lambda bi, mi: (0, mi, 0)),
        ],
        out_specs=qkv_specs + [pl.BlockSpec((1, tm, CONF_WIDTH), lambda bi, mi: (bi, mi, 0))],
        out_shape=qkv_shapes + [jax.ShapeDtypeStruct((b, s, CONF_WIDTH), F32)],
        scratch_shapes=[pltpu.VMEM((2, HEADS, tm, HEAD_DIM), F32), _weight_scratch(w_in)] + _staging_scratch(),
        compiler_params=pltpu.CompilerParams(
            dimension_semantics=("arbitrary", "arbitrary"), vmem_limit_bytes=V7X_VMEM_LIMIT),
        name="even_in_proj",
    )(x, w_in, rope_tab)


ATT_HEADS = 2
ATT_BATCH = 8


def _attn_head(head, g0_ref, g1_ref, g2_ref, o_ref, out_ref, lse_ref, stage_ref):
    seq = o_ref.shape[1]
    nb = ATT_BLOCK
    n_blocks = seq // nb
    qi = lax.broadcasted_iota(jnp.int32, (nb, 1), 0)
    kj = lax.broadcasted_iota(jnp.int32, (1, 2 * nb), 1)
    cur_only = (kj >= nb) & (kj <= qi + nb)
    band = (kj >= qi) & (kj <= qi + nb)

    for g, ref, dil in reversed(list(zip(range(3), (g0_ref, g1_ref, g2_ref), DILATIONS))):
        per_res = n_blocks // dil
        firsts = range(0, n_blocks, per_res)

        def window(j, i):
            if i in firsts:
                return ref[0, j, head, i]
            return ref[0, j, head, i - 1:i + 1].reshape(2 * nb, HEAD_DIM)

        for c0 in range(0, n_blocks, ATT_BATCH):
            blocks = list(range(c0, c0 + ATT_BATCH))
            s = [lax.dot_general(ref[0, 0, head, i], window(1, i), (((1,), (1,)), ((), ())),
                                 preferred_element_type=F32) for i in blocks]
            masks = [cur_only[:, nb:] if i in firsts else band for i in blocks]
            s = [jnp.where(mk, v, NEG_INF) for mk, v in zip(masks, s)]
            m = [jnp.max(v, axis=-1, keepdims=True) for v in s]
            ex = [jnp.exp2(v - mx) for v, mx in zip(s, m)]
            dens = [jnp.sum(e, axis=-1, keepdims=True) for e in ex]
            acc = [_dot(e.astype(BF16), window(2, i)) for e, i in zip(ex, blocks)]
            for i, a, mx, den in zip(blocks, acc, m, dens):
                out = a[:, 0:nb] / den
                lse = jnp.broadcast_to(mx + jnp.log2(den), (nb, HEAD_DIM))
                start = (i % per_res) * nb * dil + i // per_res
                if dil == SPLIT_STRIDE * SPLIT_STRIDE:
                    part, row0 = start % SPLIT_STRIDE, start // SPLIT_STRIDE
                    rows = pl.ds(row0, nb, stride=SPLIT_STRIDE)
                    stage_ref[0, part, rows, :] = out
                    stage_ref[1, part, rows, :] = lse
                else:
                    rows = pl.ds(start, nb) if dil == 1 else pl.ds(start, nb, stride=dil)
                    out_ref[g, rows, :] = out
                    lse_ref[g, rows, :] = lse
        if dil == SPLIT_STRIDE * SPLIT_STRIDE:
            for part in range(SPLIT_STRIDE):
                rows = pl.ds(part, seq // SPLIT_STRIDE, stride=SPLIT_STRIDE)
                out_ref[g, rows, :] = stage_ref[0, part]
                lse_ref[g, rows, :] = stage_ref[1, part]

    chunk = 256
    for c in range(seq // chunk):
        rows = pl.ds(c * chunk, chunk)
        lse = [lse_ref[g, rows, :] for g in range(3)]
        top = jnp.maximum(jnp.maximum(lse[0], lse[1]), lse[2])
        num = jnp.zeros((chunk, HEAD_DIM), F32)
        den = jnp.zeros((chunk, HEAD_DIM), F32)
        for g in range(3):
            a = jnp.exp2(lse[g] - top)
            num = num + a * out_ref[g, rows, :]
            den = den + a
        o_ref[0, rows, head * HEAD_DIM:(head + 1) * HEAD_DIM] = (num / den).astype(o_ref.dtype)


def _attn_kernel(g0_ref, g1_ref, g2_ref, o_ref, *scratch):
    for head in range(g0_ref.shape[2]):
        _attn_head(head, g0_ref, g1_ref, g2_ref, o_ref, *scratch)


def _attention(g0, g1, g2):
    b, _, _, s, e = g0.shape
    nb = ATT_BLOCK
    blocked = [a.reshape(b, 3, HEADS, s // nb, nb, e) for a in (g0, g1, g2)]
    spec = pl.BlockSpec((1, 3, ATT_HEADS, s // nb, nb, e), lambda bi, hi: (bi, 0, hi, 0, 0, 0))
    return pl.pallas_call(
        _attn_kernel,
        grid=(b, HEADS // ATT_HEADS),
        in_specs=[spec] * 3,
        out_specs=pl.BlockSpec((1, s, ATT_HEADS * e), lambda bi, hi: (bi, 0, hi)),
        out_shape=jax.ShapeDtypeStruct((b, s, GROUP_WIDTH), BF16),
        scratch_shapes=[pltpu.VMEM((3, s, e), F32)] * 2
        + [pltpu.VMEM((2, SPLIT_STRIDE, s // SPLIT_STRIDE, e), F32)],
        compiler_params=pltpu.CompilerParams(
            dimension_semantics=("arbitrary", "arbitrary"), vmem_limit_bytes=V7X_VMEM_LIMIT),
        name="dilated_attention",
    )(*blocked)


def _swiglu_ple(x1, p_rows, hid_ref, w_in_ref, w_out_ref, g_ref, b_ref, wp_ref, wg_ref, side_work=None):
    xb = x1.astype(BF16)
    for c in range(FFN_HIDDEN // FFN_CHUNK):
        cols = slice(c * FFN_CHUNK, (c + 1) * FFN_CHUNK)
        gate = _dot(xb, w_in_ref[:, cols])
        up = _dot(xb, w_in_ref[:, FFN_HIDDEN + c * FFN_CHUNK:FFN_HIDDEN + (c + 1) * FFN_CHUNK])
        hid_ref[:, cols] = (gate * _sigmoid(gate) * up).astype(BF16)
        if side_work is not None:
            side_work(c, gate[gate.shape[0] - SUBLANES:, 0:LANES])
    y = _dot(hid_ref[...], w_out_ref[...])
    x2 = _layer_norm(DN_ALPHA * x1 + y, g_ref[...], b_ref[...])
    emb = _dot(p_rows.astype(BF16), wp_ref[...])
    return x2 + emb * _sigmoid(_dot(x2.astype(BF16), wg_ref[...]))


def _ffn_specs(ffn_params, layer):
    w_in, w_out, ln_g, ln_b, w_proj, w_gate = ffn_params
    return [_in_hbm(), _in_hbm(), _layer_slab(ln_g, layer), _layer_slab(ln_b, layer), _in_hbm(), _in_hbm()]


def _ffn_scratch(ffn_params):
    w_in, w_out, _, _, w_proj, w_gate = ffn_params
    return [_weight_scratch(a) for a in (w_in, w_out, w_proj, w_gate)]


def _ffn_jobs(layer, ffn_in, ffn_bufs):
    w_in_hbm, w_out_hbm, g_ref, b_ref, wp_hbm, wg_hbm = ffn_in
    w_in_ref, w_out_ref, wp_ref, wg_ref = ffn_bufs
    jobs = [(w_in_hbm.at[layer], w_in_ref), (w_out_hbm.at[layer], w_out_ref),
            (wp_hbm.at[layer], wp_ref), (wg_hbm.at[layer], wg_ref)]
    return jobs, (w_in_ref, w_out_ref, g_ref, b_ref, wp_ref, wg_ref)


def _zero_from(anchor):
    bits = pltpu.bitcast(anchor, jnp.int32)
    return lax.shift_right_logical(lax.shift_right_logical(bits, 16), 16).astype(F32)[0:1, :]


def _conv_block(hist_ref, pre_ref, cw_ref, cb_ref, row_block, lane_block, after):
    first = CONV_PAD - (CONF_TAPS - 1)
    lanes = slice(lane_block * LANES, (lane_block + 1) * LANES)
    t0 = row_block * CONV_ROWS
    win = hist_ref[t0:t0 + CONV_ROWS + CONV_PAD + CONV_TAIL, lanes]
    for value in after:
        win = win + _zero_from(value)
    acc = jnp.broadcast_to(cb_ref[:, lanes], (CONV_ROWS, LANES))
    for s in range(SUBLANES):
        base, shift = divmod(first + s, SUBLANES)
        part = None
        for j in range(s, CONF_TAPS, SUBLANES):
            k0 = (j - s) + base * SUBLANES
            term = cw_ref[j:j + 1, lanes] * win[k0:k0 + CONV_ROWS + SUBLANES]
            part = term if part is None else part + term
        acc = acc + _shift_rows_up(part, shift)
    pre_ref[t0:t0 + CONV_ROWS, lanes] = acc
    return acc[0:SUBLANES]


def _conv_blocks(hist_ref, pre_ref, dst_ref, conv_refs, blocks, after):
    cw_ref, cb_ref, lg_ref, lb_ref = conv_refs
    for rb, lb in blocks:
        after = [_conv_block(hist_ref, pre_ref, cw_ref, cb_ref, rb, lb, after)]
        if lb == CONF_WIDTH // LANES - 1:
            rows = slice(rb * CONV_ROWS, (rb + 1) * CONV_ROWS)
            y = _layer_norm(pre_ref[rows, :], lg_ref[...], lb_ref[...])
            dst_ref[rows, :] = (y * _sigmoid(y)).astype(BF16)
    return after


def _even_tail_kernel(tiles_per_seq, mixer_layer, layer, attn_ref, glu0_ref, glu_next_ref, x_ref, p_ref, w_hbm,
                      mg_ref, mb_ref, cw_ref, cb_ref, lg_ref, lb_ref, *rest):
    ffn_in, rest = rest[:6], rest[6:]
    o_ref, hid_ref, hist_ref, pre_ref, conv_cur_ref, conv_next_ref, w_ref, *ffn_bufs, stage_ref, sem_ref = rest
    tm = x_ref.shape[0]
    step = pl.program_id(0)
    ffn_jobs, ffn_refs = _ffn_jobs(layer, ffn_in, ffn_bufs)

    @pl.when(step == 0)
    def _():
        _fetch_weights_bf16([(w_hbm.at[mixer_layer], w_ref)] + ffn_jobs, stage_ref, sem_ref)

    conv_refs = (cw_ref, cb_ref, lg_ref, lb_ref)
    blocks = [(rb, lb) for rb in range(tm // CONV_ROWS) for lb in range(CONF_WIDTH // LANES)]

    def stage(glu_ref, conv_tile):
        hist_ref[CONV_PAD:CONV_PAD + tm, :] = glu_ref[...]
        return (conv_tile + 1) % tiles_per_seq != 0

    def carry(keep):
        tail = hist_ref[tm:tm + CONV_PAD, :]
        hist_ref[0:CONV_PAD, :] = jnp.where(keep, tail, 0.0)

    @pl.when(step == 0)
    def _():
        hist_ref[...] = jnp.zeros(hist_ref.shape, F32)
        keep = stage(glu0_ref, step)
        zero = jnp.zeros((SUBLANES, LANES), F32)
        _conv_blocks(hist_ref, pre_ref, conv_cur_ref, conv_refs, blocks, [zero])
        carry(keep)

    mix = _dot(attn_ref[...], w_ref[0:GROUP_WIDTH, :]) + _dot(conv_cur_ref[...], w_ref[GROUP_WIDTH:, :])
    x1 = _layer_norm(DN_ALPHA * x_ref[...] + mix, mg_ref[...], mb_ref[...])

    keep = stage(glu_next_ref, step + 1)
    chunks = FFN_HIDDEN // FFN_CHUNK
    per_chunk = -(-len(blocks) // chunks)
    state = {"after": []}

    def side_work(c, anchor):
        todo = blocks[c * per_chunk:(c + 1) * per_chunk]
        state["after"] = _conv_blocks(hist_ref, pre_ref, conv_next_ref, conv_refs, todo, [anchor] + state["after"])

    o_ref[...] = _swiglu_ple(x1, p_ref[...], hid_ref, *ffn_refs, side_work=side_work)
    carry(keep)
    conv_cur_ref[...] = conv_next_ref[...]


def _even_tail(attn, glu, x2d, p_all, w_out, conv_params, mixer_layer, ln_g, ln_b, ffn_params, layer, tiles_per_seq):
    n, d = x2d.shape
    tm = ROW_TILE
    last = n // tm - 1
    return pl.pallas_call(
        functools.partial(_even_tail_kernel, tiles_per_seq, mixer_layer, layer),
        grid=(n // tm,),
        in_specs=[
            pl.BlockSpec((tm, GROUP_WIDTH), lambda i: (i, 0)),
            pl.BlockSpec((tm, CONF_WIDTH), lambda i: (0, 0)),
            pl.BlockSpec((tm, CONF_WIDTH), lambda i: (jnp.minimum(i + 1, last), 0)),
            pl.BlockSpec((tm, d), lambda i: (i, 0)),
            pl.BlockSpec((None, tm, PLE_DIM), lambda i: (layer, i, 0)),
            _in_hbm(), _layer_slab(ln_g, layer), _layer_slab(ln_b, layer),
        ] + [_layer_slab(a, mixer_layer) for a in conv_params] + _ffn_specs(ffn_params, layer),
        out_specs=pl.BlockSpec((tm, d), lambda i: (i, 0)),
        out_shape=jax.ShapeDtypeStruct((n, d), F32),
        scratch_shapes=[pltpu.VMEM((tm, FFN_HIDDEN), BF16),
                        pltpu.VMEM((CONV_PAD + tm + CONV_TAIL, CONF_WIDTH), F32),
                        pltpu.VMEM((tm, CONF_WIDTH), F32),
                        pltpu.VMEM((tm, CONF_WIDTH), BF16), pltpu.VMEM((tm, CONF_WIDTH), BF16),
                        _weight_scratch(w_out)] + _ffn_scratch(ffn_params) + _staging_scratch(),
        compiler_params=pltpu.CompilerParams(
            dimension_semantics=("arbitrary",), vmem_limit_bytes=V7X_VMEM_LIMIT),
        name="conformer_out_proj_swiglu_ple",
    )(attn, glu, glu, x2d, p_all, w_out, ln_g, ln_b, *conv_params, *ffn_params)


SHORT_CARRY = 8


def _odd_layer_kernel(mixer_layer, layer, x_ref, p_ref, w_in_hbm, cw_ref, w_out_hbm, mg_ref, mb_ref, *rest):
    ffn_in, rest = rest[:6], rest[6:]
    o_ref, gate_ref, mix_ref, hid_ref, w_in_ref, w_out_ref, *ffn_bufs, stage_ref, sem_ref = rest
    ffn_jobs, ffn_refs = _ffn_jobs(layer, ffn_in, ffn_bufs)

    @pl.when((pl.program_id(0) == 0) & (pl.program_id(1) == 0))
    def _():
        mixer_jobs = [(w_in_hbm.at[mixer_layer], w_in_ref), (w_out_hbm.at[mixer_layer], w_out_ref)]
        _fetch_weights_bf16(mixer_jobs + ffn_jobs, stage_ref, sem_ref)

    tm = x_ref.shape[1]
    width = D_MODEL
    chunk = 512

    @pl.when(pl.program_id(1) == 0)
    def _():
        gate_ref[0:SHORT_CARRY, :] = jnp.zeros((SHORT_CARRY, width), F32)

    xb = x_ref[0].astype(BF16)
    for c in range(width // chunk):
        cols = slice(c * chunk, (c + 1) * chunk)
        cg = _dot(xb, w_in_ref[:, width + c * chunk:width + (c + 1) * chunk])
        hh = _dot(xb, w_in_ref[:, 2 * width + c * chunk:2 * width + (c + 1) * chunk])
        gate_ref[SHORT_CARRY:, cols] = cg * hh
        y = jnp.zeros((tm, chunk), F32)
        for j in range(SHORT_TAPS):
            off = SHORT_CARRY - (SHORT_TAPS - 1) + j
            y = y + cw_ref[j:j + 1, cols] * gate_ref[off:off + tm, cols]
        bg = _dot(xb, w_in_ref[:, cols])
        mix_ref[:, cols] = (bg * y).astype(BF16)
        gate_ref[0:SHORT_CARRY, cols] = gate_ref[tm:tm + SHORT_CARRY, cols]
    mix = _dot(mix_ref[...], w_out_ref[...])
    x1 = _layer_norm(DN_ALPHA * x_ref[0] + mix, mg_ref[...], mb_ref[...])
    o_ref[0] = _swiglu_ple(x1, p_ref[...], hid_ref, *ffn_refs)


def _odd_layer(x, p_all, w_in, conv_w, w_out, mixer_layer, ln_g, ln_b, ffn_params, layer):
    b, s, d = x.shape
    tm = ROW_TILE
    tiles = s // tm
    return pl.pallas_call(
        functools.partial(_odd_layer_kernel, mixer_layer, layer),
        grid=(b, tiles),
        in_specs=[
            pl.BlockSpec((1, tm, d), lambda bi, mi: (bi, mi, 0)),
            pl.BlockSpec((None, tm, PLE_DIM), lambda bi, mi: (layer, bi * tiles + mi, 0)),
            _in_hbm(), _layer_slab(conv_w, mixer_layer), _in_hbm(),
            _layer_slab(ln_g, layer), _layer_slab(ln_b, layer),
        ] + _ffn_specs(ffn_params, layer),
        out_specs=pl.BlockSpec((1, tm, d), lambda bi, mi: (bi, mi, 0)),
        out_shape=jax.ShapeDtypeStruct((b, s, d), F32),
        scratch_shapes=[pltpu.VMEM((SHORT_CARRY + tm, d), F32), pltpu.VMEM((tm, d), BF16),
                        pltpu.VMEM((tm, FFN_HIDDEN), BF16), _weight_scratch(w_in), _weight_scratch(w_out)]
        + _ffn_scratch(ffn_params) + _staging_scratch(),
        compiler_params=pltpu.CompilerParams(
            dimension_semantics=("arbitrary", "arbitrary"), vmem_limit_bytes=V7X_VMEM_LIMIT),
        name="short_conv_mixer_swiglu_ple",
    )(x, p_all, w_in, conv_w, w_out, ln_g, ln_b, *ffn_params)


def _rope_tables(seq):
    half = HEAD_DIM // 2
    inv = ROPE_THETA ** (-np.arange(half, dtype=np.float64) / half)
    ang = np.arange(seq, dtype=np.float64)[:, None] * inv[None, :]
    cos = np.concatenate([np.cos(ang), np.cos(ang)], axis=-1)
    sin = np.concatenate([-np.sin(ang), np.sin(ang)], axis=-1)
    scale = HEAD_DIM ** -0.5 * np.log2(np.e)
    return np.stack([cos * scale, sin * scale, cos, sin]).astype(np.float32)


def kernel(x, p, even_w_in, even_w_out, conf_conv_w, conf_conv_b, conf_ln_g, conf_ln_b, odd_w_in, odd_conv_w, odd_w_out, ln_mix_g, ln_mix_b, ln_ffn_g, ln_ffn_b, ffn_w_in, ffn_w_out, ple_w_proj, ple_w_gate):
    b, s, d = x.shape
    depth = p.shape[0]
    rope_tab = _rope_tables(s)
    p_all = p.reshape(depth, b * s, PLE_DIM)
    conf_conv_b, conf_ln_g, conf_ln_b, ln_mix_g, ln_mix_b, ln_ffn_g, ln_ffn_b = (
        _rows(v) for v in (conf_conv_b, conf_ln_g, conf_ln_b, ln_mix_g, ln_mix_b, ln_ffn_g, ln_ffn_b))
    for i in range(depth):
        j = i // 2
        ffn_params = (ffn_w_in, ffn_w_out, ln_ffn_g, ln_ffn_b, ple_w_proj, ple_w_gate)
        if i % 2 == 0:
            g0, g1, g2, glu = _even_in(x, even_w_in, j, rope_tab)
            attn = _attention(g0, g1, g2)
            conv_params = (conf_conv_w, conf_conv_b, conf_ln_g, conf_ln_b)
            x = _even_tail(attn.reshape(b * s, -1), glu.reshape(b * s, -1), x.reshape(b * s, d), p_all,
                           even_w_out, conv_params, j, ln_mix_g, ln_mix_b, ffn_params, i,
                           s // ROW_TILE).reshape(b, s, d)
        else:
            x = _odd_layer(x, p_all, odd_w_in, odd_conv_w, odd_w_out, j, ln_mix_g, ln_mix_b, ffn_params, i)
    return x
```
